```python
import math
import jax, jax.numpy as jnp
from jax import lax
import numpy as np

D_MODEL = 1024
BATCH = 1
SEQ = 16384
DEPTH = 2
DEC_BATCH = 16
DEC_SEQ = 2048
PAST_LEN = 128

N_MIXERS = 2
N_HEADS = 8
N_KV_HEADS = 2
HEAD_DIM = 128
GQA_GROUP = N_HEADS // N_KV_HEADS
Q_DIM = N_HEADS * HEAD_DIM
KV_DIM = N_KV_HEADS * HEAD_DIM
QKV_DIM = Q_DIM + 2 * KV_DIM
D_FF = -(-8 * D_MODEL // (3 * 256)) * 256
GRID_W = 64
AXIS_DIM = HEAD_DIM // 2
ROPE_THETA = 10000.0
Q_BLOCK = 128
WINDOW = 128
N_BUCKETS = 32
MAX_DISTANCE = 128
LN_EPS = 1e-5
RMS_EPS = 1e-6
DEEPNORM_ALPHA = (2.0 * DEPTH) ** 0.25
DEEPNORM_BETA = (8.0 * DEPTH) ** -0.25
N_A_LAYERS = (DEPTH + N_MIXERS - 1) // N_MIXERS
N_B_LAYERS = DEPTH // N_MIXERS

kernel_name = "hybrid_axial_global_windowed_sink_encoder"


def layer_norm(x, g, b):
    xf = x.astype(jnp.float32)
    mu = jnp.mean(xf, axis=-1, keepdims=True)
    var = jnp.mean(jnp.square(xf - mu), axis=-1, keepdims=True)
    return ((xf - mu) * lax.rsqrt(var + LN_EPS)).astype(x.dtype) * g + b


def rms_norm(x, g):
    xf = x.astype(jnp.float32)
    ms = jnp.mean(jnp.square(xf), axis=-1, keepdims=True)
    return (xf * lax.rsqrt(ms + RMS_EPS)).astype(x.dtype) * g


def axial_rope_tables(seq_len):
    rows_n = seq_len // GRID_W
    rows = jnp.repeat(jnp.arange(rows_n, dtype=jnp.float32), GRID_W)
    cols = jnp.tile(jnp.arange(GRID_W, dtype=jnp.float32), rows_n)
    inv_freq = ROPE_THETA ** (-jnp.arange(0, AXIS_DIM, 2, dtype=jnp.float32) / AXIS_DIM)
    ang = jnp.stack([rows[:, None] * inv_freq, cols[:, None] * inv_freq], axis=1)
    return jnp.cos(ang), jnp.sin(ang)


def apply_axial_rope(x, cos, sin):
    B, S, nh, _ = x.shape
    xr = x.reshape(B, S, nh, 2, 2, AXIS_DIM // 2)
    x1, x2 = xr[..., 0, :], xr[..., 1, :]
    c = cos[None, :, None].astype(x.dtype)
    s = sin[None, :, None].astype(x.dtype)
    out = jnp.stack([x1 * c - x2 * s, x2 * c + x1 * s], axis=-2)
    return out.reshape(B, S, nh, HEAD_DIM)


def t5_bucket(rel):
    nb = N_BUCKETS // 2
    max_exact = nb // 2
    base = (rel > 0).astype(jnp.int32) * nb
    n = jnp.abs(rel)
    nf = jnp.maximum(n, max_exact).astype(jnp.float32)
    large = max_exact + (jnp.log(nf / max_exact) / math.log(MAX_DISTANCE / max_exact)
                         * (nb - max_exact)).astype(jnp.int32)
    large = jnp.minimum(large, nb - 1)
    return base + jnp.where(n < max_exact, n, large)


def split_qkv(x, w_qkv):
    B, S, _ = x.shape
    qkv = x @ w_qkv
    q = qkv[..., :Q_DIM].reshape(B, S, N_HEADS, HEAD_DIM)
    k = qkv[..., Q_DIM:Q_DIM + KV_DIM].reshape(B, S, N_KV_HEADS, HEAD_DIM)
    v = qkv[..., Q_DIM + KV_DIM:].reshape(B, S, N_KV_HEADS, HEAD_DIM)
    return q, k, v


def global_axial_attention(x, w_qkv, q_gain, k_gain, w_o):
    B, S, _ = x.shape
    q, k, v = split_qkv(x, w_qkv)
    q = rms_norm(q, q_gain)
    k = rms_norm(k, k_gain)
    cos, sin = axial_rope_tables(S)
    q = apply_axial_rope(q, cos, sin)
    k = apply_axial_rope(k, cos, sin)
    nb = S // Q_BLOCK
    qb = q.reshape(B, nb, Q_BLOCK, N_KV_HEADS, GQA_GROUP, HEAD_DIM).transpose(1, 0, 2, 3, 4, 5)
    scale = HEAD_DIM ** -0.5

    def one_block(q_blk):
        s = jnp.einsum('bqkgd,bskd->bkgqs', q_blk, k).astype(jnp.float32) * scale
        p = jax.nn.softmax(s, axis=-1).astype(v.dtype)
        return jnp.einsum('bkgqs,bskd->bqkgd', p, v)

    o = lax.map(one_block, qb)
    o = o.transpose(1, 0, 2, 3, 4, 5).reshape(B, S, Q_DIM)
    return o @ w_o


def windowed_sink_attention(x, w_qkv, sink, w_o, rel_bias_table):
    B, S, _ = x.shape
    q, k, v = split_qkv(x, w_qkv)
    nb = S // Q_BLOCK
    C = 3 * Q_BLOCK
    pad = ((0, 0), (WINDOW, WINDOW), (0, 0), (0, 0))
    kb = jnp.pad(k, pad).reshape(B, nb + 2, Q_BLOCK, N_KV_HEADS, HEAD_DIM)
    vb = jnp.pad(v, pad).reshape(B, nb + 2, Q_BLOCK, N_KV_HEADS, HEAD_DIM)
    kwin = jnp.concatenate([kb[:, :-2], kb[:, 1:-1], kb[:, 2:]], axis=2)
    vwin = jnp.concatenate([vb[:, :-2], vb[:, 1:-1], vb[:, 2:]], axis=2)
    qb = q.reshape(B, nb, Q_BLOCK, N_KV_HEADS, GQA_GROUP, HEAD_DIM)

    r = jnp.arange(Q_BLOCK)[:, None]
    c = jnp.arange(C)[None, :]
    rel = (c - WINDOW) - r
    bias = rel_bias_table[t5_bucket(rel)].astype(jnp.float32)
    bias = bias.transpose(2, 0, 1).reshape(N_KV_HEADS, GQA_GROUP, Q_BLOCK, C)
    kpos = jnp.arange(nb)[:, None] * Q_BLOCK - WINDOW + c
    valid = (kpos >= 0) & (kpos < S)
    mask = valid[:, None, :] & (jnp.abs(rel) <= WINDOW)[None]

    scale = HEAD_DIM ** -0.5
    s = jnp.einsum('bnqkgd,bnckd->bnkgqc', qb, kwin).astype(jnp.float32) * scale + bias
    s = jnp.where(mask[None, :, None, None], s, -jnp.inf)
    sink_col = jnp.broadcast_to(
        sink.astype(jnp.float32).reshape(N_KV_HEADS, GQA_GROUP)[None, None, :, :, None, None],
        s.shape[:-1] + (1,))
    p = jax.nn.softmax(jnp.concatenate([s, sink_col], axis=-1), axis=-1)[..., :C]
    o = jnp.einsum('bnkgqc,bnckd->bnqkgd', p.astype(v.dtype), vwin).reshape(B, S, Q_DIM)
    return o @ w_o


def swiglu(x, w_gate, w_up, w_down):
    return (jax.nn.silu(x @ w_gate) * (x @ w_up)) @ w_down


def trunk(x, a_w_qkv, a_q_gain, a_k_gain, a_w_o, b_w_qkv, b_sink, b_w_o, rel_bias_table,
          ln1_g, ln1_b, w_gate, w_up, w_down, ln2_g, ln2_b):
    for i in range(DEPTH):
        j = i // N_MIXERS
        if i % N_MIXERS == 0:
            h = global_axial_attention(x, a_w_qkv[j], a_q_gain[j], a_k_gain[j], a_w_o[j])
        else:
            h = windowed_sink_attention(x, b_w_qkv[j], b_sink[j], b_w_o[j], rel_bias_table)
        x = layer_norm(DEEPNORM_ALPHA * x + h, ln1_g[i], ln1_b[i])
        x = layer_norm(DEEPNORM_ALPHA * x + swiglu(x, w_gate[i], w_up[i], w_down[i]), ln2_g[i], ln2_b[i])
    return x


def _qkv_init(k, n):
    w = jax.random.normal(k, (n, D_MODEL, QKV_DIM), jnp.float32) * D_MODEL ** -0.5
    col_scale = jnp.concatenate([jnp.ones((Q_DIM + KV_DIM,), jnp.float32),
                                 jnp.full((KV_DIM,), DEEPNORM_BETA, jnp.float32)])
    return w * col_scale


def setup_inputs(seed: int = 0) -> dict:
    key = jax.random.key(seed)
    ks = jax.random.split(key, 20)
    f32 = jnp.float32
    nrm = lambda k, shape, s: jax.random.normal(k, shape, f32) * s
    return {
        "x_prompt": nrm(ks[0], (BATCH, SEQ, D_MODEL), 1.0),
        "x_sample": nrm(ks[1], (DEC_BATCH, DEC_SEQ, D_MODEL), 1.0),
        "a_w_qkv": _qkv_init(ks[2], N_A_LAYERS),
        "a_q_gain": 1.0 + nrm(ks[3], (N_A_LAYERS, HEAD_DIM), 0.02),
        "a_k_gain": 1.0 + nrm(ks[4], (N_A_LAYERS, HEAD_DIM), 0.02),
        "a_w_o": nrm(ks[5], (N_A_LAYERS, Q_DIM, D_MODEL), DEEPNORM_BETA * Q_DIM ** -0.5),
        "b_w_qkv": _qkv_init(ks[6], N_B_LAYERS),
        "b_sink": nrm(ks[7], (N_B_LAYERS, N_HEADS), 0.5),
        "b_w_o": nrm(ks[8], (N_B_LAYERS, Q_DIM, D_MODEL), DEEPNORM_BETA * Q_DIM ** -0.5),
        "rel_bias_table": nrm(ks[9], (N_BUCKETS, N_HEADS), 0.5),
        "ln1_g": 1.0 + nrm(ks[10], (DEPTH, D_MODEL), 0.02),
        "ln1_b": nrm(ks[11], (DEPTH, D_MODEL), 0.02),
        "w_gate": nrm(ks[12], (DEPTH, D_MODEL, D_FF), D_MODEL ** -0.5),
        "w_up": nrm(ks[13], (DEPTH, D_MODEL, D_FF), D_MODEL ** -0.5),
        "w_down": nrm(ks[14], (DEPTH, D_FF, D_MODEL), DEEPNORM_BETA * D_FF ** -0.5),
        "ln2_g": 1.0 + nrm(ks[15], (DEPTH, D_MODEL), 0.02),
        "ln2_b": nrm(ks[16], (DEPTH, D_MODEL), 0.02),
    }


def reference(x_prompt, x_sample, a_w_qkv, a_q_gain, a_k_gain, a_w_o, b_w_qkv, b_sink, b_w_o,
              rel_bias_table, ln1_g, ln1_b, w_gate, w_up, w_down, ln2_g, ln2_b):
    y_prompt = trunk(x_prompt, a_w_qkv, a_q_gain, a_k_gain, a_w_o, b_w_qkv, b_sink, b_w_o,
                     rel_bias_table, ln1_g, ln1_b, w_gate, w_up, w_down, ln2_g, ln2_b)
    y_sample = trunk(x_sample, a_w_qkv, a_q_gain, a_k_gain, a_w_o, b_w_qkv, b_sink, b_w_o,
                     rel_bias_table, ln1_g, ln1_b, w_gate, w_up, w_down, ln2_g, ln2_b)
    return (y_prompt, y_sample)
```

```python
import functools
import math

import jax
import jax.numpy as jnp
from jax import lax
from jax.experimental import pallas as pl
from jax.experimental.pallas import tpu as pltpu

D_MODEL = 1024
DEPTH = 2
N_HEADS = 8
N_KV_HEADS = 2
HEAD_DIM = 128
GQA_GROUP = N_HEADS // N_KV_HEADS
Q_DIM = N_HEADS * HEAD_DIM
KV_DIM = N_KV_HEADS * HEAD_DIM
QKV_DIM = Q_DIM + 2 * KV_DIM
D_FF = 2816
GRID_W = 64
AXIS_DIM = HEAD_DIM // 2
ROPE_THETA = 10000.0
WINDOW = 128
N_BUCKETS = 32
MAX_DISTANCE = 128
LN_EPS = 1e-5
RMS_EPS = 1e-6
DEEPNORM_ALPHA = (2.0 * DEPTH) ** 0.25
LOG2E = math.log2(math.e)
Q_PRESCALE = HEAD_DIM ** -0.5 * LOG2E
MASKED = -1e30

V7X_VMEM_LIMIT_BYTES = 56 * 1024 * 1024

BF16 = jnp.bfloat16
F32 = jnp.float32


def _params(sem):
    return pltpu.CompilerParams(dimension_semantics=sem, vmem_limit_bytes=V7X_VMEM_LIMIT_BYTES)


def _const_spec(shape):
    nd = len(shape)
    return pl.BlockSpec(shape, lambda *_: (0,) * nd, pipeline_mode=pl.Buffered(1))


def _qkv_kernel(x_ref, wt_ref, gq_ref, gk_ref, cos_ref, sin_ref, qt_ref, k_ref, vt_ref, *,
                norm_rope, bq):
    tm = x_ref.shape[1]
    xb = x_ref[0].astype(BF16)
    qkvt = lax.dot_general(wt_ref[...], xb, (((1,), (1,)), ((), ())),
                           preferred_element_type=F32)

    def norm_rope_slab(slab, gain):
        ms = jnp.mean(slab * slab, axis=0, keepdims=True)
        slab = slab * lax.rsqrt(ms + RMS_EPS) * gain
        h = AXIS_DIM // 2
        partner = jnp.concatenate(
            [slab[h:2 * h], slab[0:h], slab[3 * h:4 * h], slab[2 * h:3 * h]], axis=0)
        return slab * cos_ref[...] + partner * sin_ref[...]

    for head in range(N_HEADS):
        slab = qkvt[head * HEAD_DIM:(head + 1) * HEAD_DIM]
        if norm_rope:
            slab = norm_rope_slab(slab, gq_ref[...])
        else:
            slab = slab * Q_PRESCALE
        slab = slab.astype(BF16)
        kvh, g = divmod(head, GQA_GROUP)
        for j in range(tm // bq):
            col = (j * GQA_GROUP + g) * bq
            qt_ref[0, kvh, :, col:col + bq] = slab[:, j * bq:(j + 1) * bq]
    for kvh in range(N_KV_HEADS):
        slab = qkvt[Q_DIM + kvh * HEAD_DIM:Q_DIM + (kvh + 1) * HEAD_DIM]
        if norm_rope:
            slab = norm_rope_slab(slab, gk_ref[...])
        k_ref[0, :, kvh * HEAD_DIM:(kvh + 1) * HEAD_DIM] = slab.T.astype(BF16)
    vt_ref[0] = qkvt[Q_DIM + KV_DIM:].astype(BF16)


def _qkv_project(x, wt, gq, gk, cos_t, sin_t, *, norm_rope, bq, tm):
    B, S, _ = x.shape
    kern = functools.partial(_qkv_kernel, norm_rope=norm_rope, bq=bq)
    return pl.pallas_call(
        kern,
        grid=(B, S // tm),
        in_specs=[
            pl.BlockSpec((1, tm, D_MODEL), lambda b, i: (b, i, 0)),
            _const_spec((QKV_DIM, D_MODEL)),
            _const_spec((HEAD_DIM, tm)),
            _const_spec((HEAD_DIM, tm)),
            pl.BlockSpec((HEAD_DIM, tm), lambda b, i: (0, i)),
            pl.BlockSpec((HEAD_DIM, tm), lambda b, i: (0, i)),
        ],
        out_specs=[
            pl.BlockSpec((1, N_KV_HEADS, HEAD_DIM, GQA_GROUP * tm), lambda b, i: (b, 0, 0, i)),
            pl.BlockSpec((1, tm, KV_DIM), lambda b, i: (b, i, 0)),
            pl.BlockSpec((1, KV_DIM, tm), lambda b, i: (b, 0, i)),
        ],
        out_shape=[
            jax.ShapeDtypeStruct((B, N_KV_HEADS, HEAD_DIM, GQA_GROUP * S), BF16),
            jax.ShapeDtypeStruct((B, S, KV_DIM), BF16),
            jax.ShapeDtypeStruct((B, KV_DIM, S), BF16),
        ],
        compiler_params=_params(("parallel", "parallel")),
        name="qkv_project",
    )(x, wt, gq, gk, cos_t, sin_t)


def _store_heads(o_ref, out_t, bq):
    for g in range(GQA_GROUP):
        o_ref[0, :, g * HEAD_DIM:(g + 1) * HEAD_DIM] = out_t[:, g * bq:(g + 1) * bq].T.astype(BF16)


def _global_attn_kernel(qt_ref, k_ref, vt_ref, o_ref, acc_ref, *, bq, bkc, n_split):
    S = k_ref.shape[1]
    nq = GQA_GROUP * bq
    w = nq // n_split
    acc_ref[...] = jnp.zeros_like(acc_ref)

    def body(c, carry):
        ms, ls = carry
        start = pl.multiple_of(c * bkc, bkc)
        k = k_ref[0, pl.ds(start, bkc), :]
        vt = vt_ref[0, :, pl.ds(start, bkc)]
        new_ms, new_ls = [], []
        for h in range(n_split):
            q = qt_ref[0, 0, :, h * w:(h + 1) * w]
            s = jnp.dot(k, q, preferred_element_type=F32)
            m_new = jnp.maximum(ms[h], jnp.max(s, axis=0, keepdims=True))
            alpha = jnp.exp2(ms[h] - m_new)
            p = jnp.exp2(s - m_new)
            new_ls.append(alpha * ls[h] + jnp.sum(p, axis=0, keepdims=True))
            new_ms.append(m_new)
            pv = jnp.dot(vt, p.astype(BF16), preferred_element_type=F32)
            acc_ref[:, h * w:(h + 1) * w] = alpha * acc_ref[:, h * w:(h + 1) * w] + pv
        return tuple(new_ms), tuple(new_ls)

    init = (tuple(jnp.full((1, w), MASKED, F32) for _ in range(n_split)),
            tuple(jnp.zeros((1, w), F32) for _ in range(n_split)))
    _, ls = lax.fori_loop(0, S // bkc, body, init)
    l = jnp.concatenate(ls, axis=1)
    _store_heads(o_ref, acc_ref[...] / l, bq)


def _global_attention(qt, k, vt, *, bq, bkc, n_split):
    B, S, _ = k.shape
    nq = GQA_GROUP * bq
    kern = functools.partial(_global_attn_kernel, bq=bq, bkc=bkc, n_split=n_split)
    return pl.pallas_call(
        kern,
        grid=(B, N_KV_HEADS, S // bq),
        in_specs=[
            pl.BlockSpec((1, 1, HEAD_DIM, nq), lambda b, h, i: (b, h, 0, i)),
            pl.BlockSpec((1, S, HEAD_DIM), lambda b, h, i: (b, 0, h)),
            pl.BlockSpec((1, HEAD_DIM, S), lambda b, h, i: (b, h, 0)),
        ],
        out_specs=pl.BlockSpec((1, bq, GQA_GROUP * HEAD_DIM), lambda b, h, i: (b, i, h)),
        out_shape=jax.ShapeDtypeStruct((B, S, Q_DIM), BF16),
        scratch_shapes=[pltpu.VMEM((HEAD_DIM, nq), F32)],
        compiler_params=_params(("parallel", "parallel", "arbitrary")),
        name="global_attention",
    )(qt, k, vt)


def _window_attn_kernel(qt_ref, kp_ref, kc_ref, kn_ref, vp_ref, vc_ref, vn_ref, bias_ref,
                        sink_ref, o_ref):
    n = pl.program_id(2)
    nb = pl.num_programs(2)
    q = qt_ref[0, 0]
    kw = jnp.concatenate([kp_ref[0], kc_ref[0], kn_ref[0]], axis=0)
    vw = jnp.concatenate([vp_ref[0], vc_ref[0], vn_ref[0]], axis=1)
    s = jnp.dot(kw, q, preferred_element_type=F32) + bias_ref[0]
    row = lax.broadcasted_iota(jnp.int32, s.shape, 0)
    outside = ((row < WINDOW) & (n == 0)) | ((row >= 2 * WINDOW) & (n == nb - 1))
    s = jnp.where(outside, MASKED, s)
    sink = sink_ref[0]
    m = jnp.maximum(jnp.max(s, axis=0, keepdims=True), sink)
    p = jnp.exp2(s - m)
    l = jnp.sum(p, axis=0, keepdims=True) + jnp.exp2(sink - m)
    out_t = jnp.dot(vw, p.astype(BF16), preferred_element_type=F32) / l
    _store_heads(o_ref, out_t, WINDOW)


def _window_attention(qt, k, vt, bias_t, sink_t):
    B, S, _ = k.shape
    nb = S // WINDOW
    nq = GQA_GROUP * WINDOW
    prev = lambda n: jnp.maximum(n - 1, 0)
    nxt = lambda n: jnp.minimum(n + 1, nb - 1)
    kspec = lambda f: pl.BlockSpec((1, WINDOW, HEAD_DIM), lambda b, h, n: (b, f(n), h))
    vspec = lambda f: pl.BlockSpec((1, HEAD_DIM, WINDOW), lambda b, h, n: (b, h, f(n)))
    ident = lambda n: n
    return pl.pallas_call(
        _window_attn_kernel,
        grid=(B, N_KV_HEADS, nb),
        in_specs=[
            pl.BlockSpec((1, 1, HEAD_DIM, nq), lambda b, h, n: (b, h, 0, n)),
            kspec(prev), kspec(ident), kspec(nxt),
            vspec(prev), vspec(ident), vspec(nxt),
            pl.BlockSpec((1, 3 * WINDOW, nq), lambda b, h, n: (h, 0, 0)),
            pl.BlockSpec((1, 1, nq), lambda b, h, n: (h, 0, 0)),
        ],
        out_specs=pl.BlockSpec((1, WINDOW, GQA_GROUP * HEAD_DIM), lambda b, h, n: (b, n, h)),
        out_shape=jax.ShapeDtypeStruct((B, S, Q_DIM), BF16),
        compiler_params=_params(("parallel", "parallel", "parallel")),
        name="window_attention",
    )(qt, k, k, k, vt, vt, vt, bias_t, sink_t)


def _layer_norm(y, g, b):
    mu = jnp.mean(y, axis=-1, keepdims=True)
    d = y - mu
    var = jnp.mean(d * d, axis=-1, keepdims=True)
    return d * lax.rsqrt(var + LN_EPS) * g + b


def _out_proj_kernel(x_ref, o_ref, wo_ref, g_ref, b_ref, y_ref):
    h = jnp.dot(o_ref[...], wo_ref[...], preferred_element_type=F32)
    y_ref[...] = _layer_norm(DEEPNORM_ALPHA * x_ref[...] + h, g_ref[...], b_ref[...])


def _out_proj_ln(x, o, wo, g, b, *, tm):
    T = x.shape[0]
    return pl.pallas_call(
        _out_proj_kernel,
        grid=(T // tm,),
        in_specs=[
            pl.BlockSpec((tm, D_MODEL), lambda i: (i, 0)),
            pl.BlockSpec((tm, Q_DIM), lambda i: (i, 0)),
            _const_spec((Q_DIM, D_MODEL)),
            _const_spec((1, D_MODEL)),
            _const_spec((1, D_MODEL)),
        ],
        out_specs=pl.BlockSpec((tm, D_MODEL), lambda i: (i, 0)),
        out_shape=jax.ShapeDtypeStruct((T, D_MODEL), F32),
        compiler_params=_params(("parallel",)),
        name="out_proj_ln",
    )(x, o, wo, g, b)


def _ffn_kernel(x_ref, wg_ref, wu_ref, wd_ref, g_ref, b_ref, y_ref):
    x = x_ref[...]
    xb = x.astype(BF16)
    gate = jnp.dot(xb, wg_ref[...], preferred_element_type=F32)
    up = jnp.dot(xb, wu_ref[...], preferred_element_type=F32)
    mid = (gate * jax.nn.sigmoid(gate) * up).astype(BF16)
    h = jnp.dot(mid, wd_ref[...], preferred_element_type=F32)
    y_ref[...] = _layer_norm(DEEPNORM_ALPHA * x + h, g_ref[...], b_ref[...])


def _ffn_ln(x, wg, wu, wd, g, b, *, tm):
    T = x.shape[0]
    return pl.pallas_call(
        _ffn_kernel,
        grid=(T // tm,),
        in_specs=[
            pl.BlockSpec((tm, D_MODEL), lambda i: (i, 0)),
            _const_spec((D_MODEL, D_FF)),
            _const_spec((D_MODEL, D_FF)),
            _const_spec((D_FF, D_MODEL)),
            _const_spec((1, D_MODEL)),
            _const_spec((1, D_MODEL)),
        ],
        out_specs=pl.BlockSpec((tm, D_MODEL), lambda i: (i, 0)),
        out_shape=jax.ShapeDtypeStruct((T, D_MODEL), F32),
        compiler_params=_params(("parallel",)),
        name="ffn_ln",
    )(x, wg, wu, wd, g, b)


def _rope_tables_t(seq_len):
    rows_n = seq_len // GRID_W
    rows = jnp.repeat(jnp.arange(rows_n, dtype=F32), GRID_W)
    cols = jnp.tile(jnp.arange(GRID_W, dtype=F32), rows_n)
    inv_freq = ROPE_THETA ** (-jnp.arange(0, AXIS_DIM, 2, dtype=F32) / AXIS_DIM)
    ang_r = (rows[:, None] * inv_freq).T
    ang_c = (cols[:, None] * inv_freq).T
    cos_t = jnp.concatenate([jnp.cos(ang_r)] * 2 + [jnp.cos(ang_c)] * 2, axis=0)
    sin_t = jnp.concatenate([-jnp.sin(ang_r), jnp.sin(ang_r), -jnp.sin(ang_c), jnp.sin(ang_c)],
                            axis=0)
    return cos_t, sin_t


def _t5_bucket(rel):
    nb = N_BUCKETS // 2
    max_exact = nb // 2
    base = (rel > 0).astype(jnp.int32) * nb
    n = jnp.abs(rel)
    nf = jnp.maximum(n, max_exact).astype(F32)
    large = max_exact + (jnp.log(nf / max_exact) / math.log(MAX_DISTANCE / max_exact)
                         * (nb - max_exact)).astype(jnp.int32)
    large = jnp.minimum(large, nb - 1)
    return base + jnp.where(n < max_exact, n, large)


def _window_bias_t(rel_bias_table):
    r = jnp.arange(WINDOW)[None, :]
    c = jnp.arange(3 * WINDOW)[:, None]
    rel = (c - WINDOW) - r
    bias = rel_bias_table[_t5_bucket(rel)].astype(F32) * LOG2E
    bias = jnp.where((jnp.abs(rel) <= WINDOW)[..., None], bias, MASKED)
    bias = bias.transpose(2, 0, 1).reshape(N_KV_HEADS, GQA_GROUP, 3 * WINDOW, WINDOW)
    return bias.transpose(0, 2, 1, 3).reshape(N_KV_HEADS, 3 * WINDOW, GQA_GROUP * WINDOW)


def _trunk(x, a_w_qkv, a_q_gain, a_k_gain, a_w_o, b_w_qkv, b_sink, b_w_o, rel_bias_table,
           ln1_g, ln1_b, w_gate, w_up, w_down, ln2_g, ln2_b):
    B, S, _ = x.shape
    T = B * S
    tm_qkv = 512
    tm_tok = 512
    cos_t, sin_t = _rope_tables_t(S)
    for i in range(DEPTH):
        j = i // 2
        if i % 2 == 0:
            bq = 256
            gq = jnp.broadcast_to((a_q_gain[j] * Q_PRESCALE)[:, None], (HEAD_DIM, tm_qkv))
            gk = jnp.broadcast_to(a_k_gain[j][:, None], (HEAD_DIM, tm_qkv))
            qt, k, vt = _qkv_project(x, a_w_qkv[j].T.astype(BF16), gq, gk, cos_t, sin_t,
                                     norm_rope=True, bq=bq, tm=tm_qkv)
            o = _global_attention(qt, k, vt, bq=bq, bkc=512, n_split=2)
            wo = a_w_o[j]
        else:
            ones = jnp.ones((HEAD_DIM, tm_qkv), F32)
            qt, k, vt = _qkv_project(x, b_w_qkv[j].T.astype(BF16), ones, ones, cos_t, sin_t,
                                     norm_rope=False, bq=WINDOW, tm=tm_qkv)
            bias_t = _window_bias_t(rel_bias_table)
            sink_t = jnp.repeat(b_sink[j].astype(F32) * LOG2E, WINDOW).reshape(
                N_KV_HEADS, 1, GQA_GROUP * WINDOW)
            o = _window_attention(qt, k, vt, bias_t, sink_t)
            wo = b_w_o[j]
        x2 = _out_proj_ln(x.reshape(T, D_MODEL), o.reshape(T, Q_DIM), wo.astype(BF16),
                          ln1_g[i][None], ln1_b[i][None], tm=tm_tok)
        x2 = _ffn_ln(x2, w_gate[i].astype(BF16), w_up[i].astype(BF16), w_down[i].astype(BF16),
                     ln2_g[i][None], ln2_b[i][None], tm=tm_tok)
        x = x2.reshape(B, S, D_MODEL)
    return x


def kernel(x_prompt, x_sample, a_w_qkv, a_q_gain, a_k_gain, a_w_o, b_w_qkv, b_sink, b_w_o,
           rel_bias_table, ln1_g, ln1_b, w_gate, w_up, w_down, ln2_g, ln2_b):
    weights = (a_w_qkv, a_q_gain, a_k_gain, a_w_o, b_w_qkv, b_sink, b_w_o, rel_bias_table,
               ln1_g, ln1_b, w_gate, w_up, w_down, ln2_g, ln2_b)
    return (_trunk(x_prompt, *weights), _trunk(x_sample, *weights))
```

```python
import functools
import math

import jax
import jax.numpy as jnp
from jax import lax
from jax.experimental import pallas as pl
from jax.experimental.pallas import tpu as pltpu

D_MODEL = 1024
DEPTH = 2
N_HEADS = 8
N_KV_HEADS = 2
HEAD_DIM = 128
GQA_GROUP = N_HEADS // N_KV_HEADS
Q_DIM = N_HEADS * HEAD_DIM
KV_DIM = N_KV_HEADS * HEAD_DIM
QKV_DIM = Q_DIM + 2 * KV_DIM
D_FF = 2816
GRID_W = 64
AXIS_DIM = HEAD_DIM // 2
ROPE_THETA = 10000.0
WINDOW = 128
N_BUCKETS = 32
MAX_DISTANCE = 128
LN_EPS = 1e-5
RMS_EPS = 1e-6
DEEPNORM_ALPHA = (2.0 * DEPTH) ** 0.25
LOG2E = math.log2(math.e)
Q_PRESCALE = HEAD_DIM ** -0.5 * LOG2E
MASKED = -1e30
SAFE_LOG2_SPAN = 48.0

V7X_VMEM_LIMIT_BYTES = 56 * 1024 * 1024

BF16 = jnp.bfloat16
F32 = jnp.float32


def _params(sem):
    return pltpu.CompilerParams(dimension_semantics=sem, vmem_limit_bytes=V7X_VMEM_LIMIT_BYTES)


def _const_spec(shape):
    nd = len(shape)
    return pl.BlockSpec(shape, lambda *_: (0,) * nd, pipeline_mode=pl.Buffered(1))


def _qkv_kernel(x_ref, wt_ref, gq_ref, gk_ref, cos_ref, sin_ref, qt_ref, k_ref, vt_ref, *norm_refs,
                norm_rope, bq):
    tm = x_ref.shape[1]
    xb = x_ref[0].astype(BF16)
    qkvt = lax.dot_general(wt_ref[...], xb, (((1,), (1,)), ((), ())),
                           preferred_element_type=F32)

    def norm_rope_slab(slab, gain):
        ms = jnp.mean(slab * slab, axis=0, keepdims=True)
        slab = slab * lax.rsqrt(ms + RMS_EPS) * gain
        h = AXIS_DIM // 2
        partner = jnp.concatenate(
            [slab[h:2 * h], slab[0:h], slab[3 * h:4 * h], slab[2 * h:3 * h]], axis=0)
        return slab * cos_ref[...] + partner * sin_ref[...]

    for head in range(N_HEADS):
        slab = qkvt[head * HEAD_DIM:(head + 1) * HEAD_DIM]
        if norm_rope:
            slab = norm_rope_slab(slab, gq_ref[...])
        else:
            slab = slab * Q_PRESCALE
        kvh, g = divmod(head, GQA_GROUP)
        if norm_rope:
            sq = jnp.sum(slab * slab, axis=0, keepdims=True)
        slab = slab.astype(BF16)
        for j in range(tm // bq):
            col = (j * GQA_GROUP + g) * bq
            qt_ref[0, kvh, :, col:col + bq] = slab[:, j * bq:(j + 1) * bq]
            if norm_rope:
                norm_refs[0][0, kvh, :, col:col + bq] = sq[:, j * bq:(j + 1) * bq]
    for kvh in range(N_KV_HEADS):
        slab = qkvt[Q_DIM + kvh * HEAD_DIM:Q_DIM + (kvh + 1) * HEAD_DIM]
        if norm_rope:
            slab = norm_rope_slab(slab, gk_ref[...])
            norm_refs[1][0, kvh] = jnp.sum(slab * slab, axis=0, keepdims=True)
        k_ref[0, :, kvh * HEAD_DIM:(kvh + 1) * HEAD_DIM] = slab.T.astype(BF16)
    vt_ref[0] = qkvt[Q_DIM + KV_DIM:].astype(BF16)


def _qkv_project(x, wt, gq, gk, cos_t, sin_t, *, norm_rope, bq, tm):
    B, S, _ = x.shape
    kern = functools.partial(_qkv_kernel, norm_rope=norm_rope, bq=bq)
    out_specs = [
        pl.BlockSpec((1, N_KV_HEADS, HEAD_DIM, GQA_GROUP * tm), lambda b, i: (b, 0, 0, i)),
        pl.BlockSpec((1, tm, KV_DIM), lambda b, i: (b, i, 0)),
        pl.BlockSpec((1, KV_DIM, tm), lambda b, i: (b, 0, i)),
    ]
    out_shape = [
        jax.ShapeDtypeStruct((B, N_KV_HEADS, HEAD_DIM, GQA_GROUP * S), BF16),
        jax.ShapeDtypeStruct((B, S, KV_DIM), BF16),
        jax.ShapeDtypeStruct((B, KV_DIM, S), BF16),
    ]
    if norm_rope:
        out_specs += [
            pl.BlockSpec((1, N_KV_HEADS, 1, GQA_GROUP * tm), lambda b, i: (b, 0, 0, i)),
            pl.BlockSpec((1, N_KV_HEADS, 1, tm), lambda b, i: (b, 0, 0, i)),
        ]
        out_shape += [
            jax.ShapeDtypeStruct((B, N_KV_HEADS, 1, GQA_GROUP * S), F32),
            jax.ShapeDtypeStruct((B, N_KV_HEADS, 1, S), F32),
        ]
    return pl.pallas_call(
        kern,
        grid=(B, S // tm),
        in_specs=[
            pl.BlockSpec((1, tm, D_MODEL), lambda b, i: (b, i, 0)),
            _const_spec((QKV_DIM, D_MODEL)),
            _const_spec((HEAD_DIM, tm)),
            _const_spec((HEAD_DIM, tm)),
            pl.BlockSpec((HEAD_DIM, tm), lambda b, i: (0, i)),
            pl.BlockSpec((HEAD_DIM, tm), lambda b, i: (0, i)),
        ],
        out_specs=out_specs,
        out_shape=out_shape,
        compiler_params=_params(("parallel", "parallel")),
        name="qkv_project",
    )(x, wt, gq, gk, cos_t, sin_t)


def _store_heads(o_ref, out_t, bq):
    for g in range(GQA_GROUP):
        o_ref[0, :, g * HEAD_DIM:(g + 1) * HEAD_DIM] = out_t[:, g * bq:(g + 1) * bq].T.astype(BF16)


def _global_attn_kernel(qt_ref, k_ref, vt_ref, o_ref, acc_ref, *, bq, bkc, n_split):
    S = k_ref.shape[1]
    nq = GQA_GROUP * bq
    w = nq // n_split
    acc_ref[...] = jnp.zeros_like(acc_ref)

    def body(c, carry):
        ms, ls = carry
        start = pl.multiple_of(c * bkc, bkc)
        k = k_ref[0, pl.ds(start, bkc), :]
        vt = vt_ref[0, :, pl.ds(start, bkc)]
        new_ms, new_ls = [], []
        for h in range(n_split):
            q = qt_ref[0, 0, :, h * w:(h + 1) * w]
            s = jnp.dot(k, q, preferred_element_type=F32)
            m_new = jnp.maximum(ms[h], jnp.max(s, axis=0, keepdims=True))
            alpha = jnp.exp2(ms[h] - m_new)
            p = jnp.exp2(s - m_new)
            new_ls.append(alpha * ls[h] + jnp.sum(p, axis=0, keepdims=True))
            new_ms.append(m_new)
            pv = jnp.dot(vt, p.astype(BF16), preferred_element_type=F32)
            acc_ref[:, h * w:(h + 1) * w] = alpha * acc_ref[:, h * w:(h + 1) * w] + pv
        return tuple(new_ms), tuple(new_ls)

    init = (tuple(jnp.full((1, w), MASKED, F32) for _ in range(n_split)),
            tuple(jnp.zeros((1, w), F32) for _ in range(n_split)))
    _, ls = lax.fori_loop(0, S // bkc, body, init)
    l = jnp.concatenate(ls, axis=1)
    _store_heads(o_ref, acc_ref[...] / l, bq)


def _global_attn_bounded_kernel(qt_ref, m_ref, k_ref, vt_ref, o_ref, acc_ref, *, bq, bkc, n_split):
    S = k_ref.shape[1]
    nq = GQA_GROUP * bq
    w = nq // n_split
    acc_ref[...] = jnp.zeros_like(acc_ref)

    def body(c, ls):
        start = pl.multiple_of(c * bkc, bkc)
        k = k_ref[0, pl.ds(start, bkc), :]
        vt = vt_ref[0, :, pl.ds(start, bkc)]
        ss = [jnp.dot(k, qt_ref[0, 0, :, h * w:(h + 1) * w], preferred_element_type=F32)
              for h in range(n_split)]
        new_ls = []
        for h in range(n_split):
            cols = slice(h * w, (h + 1) * w)
            p = jnp.exp2(ss[h] - m_ref[0, 0, :, cols])
            new_ls.append(ls[h] + jnp.sum(p.reshape(bkc // 8, 8, w), axis=0))
            acc_ref[:, cols] += jnp.dot(vt, p.astype(BF16), preferred_element_type=F32)
        return tuple(new_ls)

    ls = lax.fori_loop(0, S // bkc, body, tuple(jnp.zeros((8, w), F32) for _ in range(n_split)),
                       unroll=2)
    l = jnp.concatenate([jnp.sum(x, axis=0, keepdims=True) for x in ls], axis=1)
    _store_heads(o_ref, acc_ref[...] / l, bq)


def _global_attention_bounded(qt, m, k, vt, *, bq, bkc, n_split):
    B, S, _ = k.shape
    nq = GQA_GROUP * bq
    kern = functools.partial(_global_attn_bounded_kernel, bq=bq, bkc=bkc, n_split=n_split)
    return pl.pallas_call(
        kern,
        grid=(B, N_KV_HEADS, S // bq),
        in_specs=[
            pl.BlockSpec((1, 1, HEAD_DIM, nq), lambda b, h, i: (b, h, 0, i)),
            pl.BlockSpec((1, 1, 1, nq), lambda b, h, i: (b, h, 0, i)),
            pl.BlockSpec((1, S, HEAD_DIM), lambda b, h, i: (b, 0, h)),
            pl.BlockSpec((1, HEAD_DIM, S), lambda b, h, i: (b, h, 0)),
        ],
        out_specs=pl.BlockSpec((1, bq, GQA_GROUP * HEAD_DIM), lambda b, h, i: (b, i, h)),
        out_shape=jax.ShapeDtypeStruct((B, S, Q_DIM), BF16),
        scratch_shapes=[pltpu.VMEM((HEAD_DIM, nq), F32)],
        compiler_params=_params(("parallel", "parallel", "arbitrary")),
        name="global_attention_bounded",
    )(qt, m, k, vt)


def _global_attention(qt, k, vt, *, bq, bkc, n_split):
    B, S, _ = k.shape
    nq = GQA_GROUP * bq
    kern = functools.partial(_global_attn_kernel, bq=bq, bkc=bkc, n_split=n_split)
    return pl.pallas_call(
        kern,
        grid=(B, N_KV_HEADS, S // bq),
        in_specs=[
            pl.BlockSpec((1, 1, HEAD_DIM, nq), lambda b, h, i: (b, h, 0, i)),
            pl.BlockSpec((1, S, HEAD_DIM), lambda b, h, i: (b, 0, h)),
            pl.BlockSpec((1, HEAD_DIM, S), lambda b, h, i: (b, h, 0)),
        ],
        out_specs=pl.BlockSpec((1, bq, GQA_GROUP * HEAD_DIM), lambda b, h, i: (b, i, h)),
        out_shape=jax.ShapeDtypeStruct((B, S, Q_DIM), BF16),
        scratch_shapes=[pltpu.VMEM((HEAD_DIM, nq), F32)],
        compiler_params=_params(("parallel", "parallel", "arbitrary")),
        name="global_attention",
    )(qt, k, vt)


def _window_attn_kernel(qt_ref, kp_ref, kc_ref, kn_ref, vp_ref, vc_ref, vn_ref, bias_ref,
                        sink_ref, o_ref):
    n = pl.program_id(2)
    nb = pl.num_programs(2)
    q = qt_ref[0, 0]
    kw = jnp.concatenate([kp_ref[0], kc_ref[0], kn_ref[0]], axis=0)
    vw = jnp.concatenate([vp_ref[0], vc_ref[0], vn_ref[0]], axis=1)
    s = jnp.dot(kw, q, preferred_element_type=F32) + bias_ref[0]
    row = lax.broadcasted_iota(jnp.int32, s.shape, 0)
    outside = ((row < WINDOW) & (n == 0)) | ((row >= 2 * WINDOW) & (n == nb - 1))
    s = jnp.where(outside, MASKED, s)
    sink = sink_ref[0]
    m = jnp.maximum(jnp.max(s, axis=0, keepdims=True), sink)
    p = jnp.exp2(s - m)
    l = jnp.sum(p, axis=0, keepdims=True) + jnp.exp2(sink - m)
    out_t = jnp.dot(vw, p.astype(BF16), preferred_element_type=F32) / l
    _store_heads(o_ref, out_t, WINDOW)


def _window_attention(qt, k, vt, bias_t, sink_t):
    B, S, _ = k.shape
    nb = S // WINDOW
    nq = GQA_GROUP * WINDOW
    prev = lambda n: jnp.maximum(n - 1, 0)
    nxt = lambda n: jnp.minimum(n + 1, nb - 1)
    kspec = lambda f: pl.BlockSpec((1, WINDOW, HEAD_DIM), lambda b, h, n: (b, f(n), h))
    vspec = lambda f: pl.BlockSpec((1, HEAD_DIM, WINDOW), lambda b, h, n: (b, h, f(n)))
    ident = lambda n: n
    return pl.pallas_call(
        _window_attn_kernel,
        grid=(B, N_KV_HEADS, nb),
        in_specs=[
            pl.BlockSpec((1, 1, HEAD_DIM, nq), lambda b, h, n: (b, h, 0, n)),
            kspec(prev), kspec(ident), kspec(nxt),
            vspec(prev), vspec(ident), vspec(nxt),
            pl.BlockSpec((1, 3 * WINDOW, nq), lambda b, h, n: (h, 0, 0)),
            pl.BlockSpec((1, 1, nq), lambda b, h, n: (h, 0, 0)),
        ],
        out_specs=pl.BlockSpec((1, WINDOW, GQA_GROUP * HEAD_DIM), lambda b, h, n: (b, n, h)),
        out_shape=jax.ShapeDtypeStruct((B, S, Q_DIM), BF16),
        compiler_params=_params(("parallel", "parallel", "parallel")),
        name="window_attention",
    )(qt, k, k, k, vt, vt, vt, bias_t, sink_t)


def _layer_norm(y, g, b):
    mu = jnp.mean(y, axis=-1, keepdims=True)
    d = y - mu
    var = jnp.mean(d * d, axis=-1, keepdims=True)
    return d * lax.rsqrt(var + LN_EPS) * g + b


def _out_proj_kernel(x_ref, o_ref, wo_ref, g_ref, b_ref, y_ref):
    h = jnp.dot(o_ref[...], wo_ref[...], preferred_element_type=F32)
    y_ref[...] = _layer_norm(DEEPNORM_ALPHA * x_ref[...] + h, g_ref[...], b_ref[...])


def _out_proj_ln(x, o, wo, g, b, *, tm):
    T = x.shape[0]
    return pl.pallas_call(
        _out_proj_kernel,
        grid=(T // tm,),
        in_specs=[
            pl.BlockSpec((tm, D_MODEL), lambda i: (i, 0)),
            pl.BlockSpec((tm, Q_DIM), lambda i: (i, 0)),
            _const_spec((Q_DIM, D_MODEL)),
            _const_spec((1, D_MODEL)),
            _const_spec((1, D_MODEL)),
        ],
        out_specs=pl.BlockSpec((tm, D_MODEL), lambda i: (i, 0)),
        out_shape=jax.ShapeDtypeStruct((T, D_MODEL), F32),
        compiler_params=_params(("parallel",)),
        name="out_proj_ln",
    )(x, o, wo, g, b)


def _ffn_kernel(x_ref, wg_ref, wu_ref, wd_ref, g_ref, b_ref, y_ref):
    x = x_ref[...]
    xb = x.astype(BF16)
    gate = jnp.dot(xb, wg_ref[...], preferred_element_type=F32)
    up = jnp.dot(xb, wu_ref[...], preferred_element_type=F32)
    mid = (gate * jax.nn.sigmoid(gate) * up).astype(BF16)
    h = jnp.dot(mid, wd_ref[...], preferred_element_type=F32)
    y_ref[...] = _layer_norm(DEEPNORM_ALPHA * x + h, g_ref[...], b_ref[...])


def _ffn_ln(x, wg, wu, wd, g, b, *, tm):
    T = x.shape[0]
    return pl.pallas_call(
        _ffn_kernel,
        grid=(T // tm,),
        in_specs=[
            pl.BlockSpec((tm, D_MODEL), lambda i: (i, 0)),
            _const_spec((D_MODEL, D_FF)),
            _const_spec((D_MODEL, D_FF)),
            _const_spec((D_FF, D_MODEL)),
            _const_spec((1, D_MODEL)),
            _const_spec((1, D_MODEL)),
        ],
        out_specs=pl.BlockSpec((tm, D_MODEL), lambda i: (i, 0)),
        out_shape=jax.ShapeDtypeStruct((T, D_MODEL), F32),
        compiler_params=_params(("parallel",)),
        name="ffn_ln",
    )(x, wg, wu, wd, g, b)


def _rope_tables_t(seq_len):
    rows_n = seq_len // GRID_W
    rows = jnp.repeat(jnp.arange(rows_n, dtype=F32), GRID_W)
    cols = jnp.tile(jnp.arange(GRID_W, dtype=F32), rows_n)
    inv_freq = ROPE_THETA ** (-jnp.arange(0, AXIS_DIM, 2, dtype=F32) / AXIS_DIM)
    ang_r = (rows[:, None] * inv_freq).T
    ang_c = (cols[:, None] * inv_freq).T
    cos_t = jnp.concatenate([jnp.cos(ang_r)] * 2 + [jnp.cos(ang_c)] * 2, axis=0)
    sin_t = jnp.concatenate([-jnp.sin(ang_r), jnp.sin(ang_r), -jnp.sin(ang_c), jnp.sin(ang_c)],
                            axis=0)
    return cos_t, sin_t


def _t5_bucket(rel):
    nb = N_BUCKETS // 2
    max_exact = nb // 2
    base = (rel > 0).astype(jnp.int32) * nb
    n = jnp.abs(rel)
    nf = jnp.maximum(n, max_exact).astype(F32)
    large = max_exact + (jnp.log(nf / max_exact) / math.log(MAX_DISTANCE / max_exact)
                         * (nb - max_exact)).astype(jnp.int32)
    large = jnp.minimum(large, nb - 1)
    return base + jnp.where(n < max_exact, n, large)


def _window_bias_t(rel_bias_table):
    r = jnp.arange(WINDOW)[None, :]
    c = jnp.arange(3 * WINDOW)[:, None]
    rel = (c - WINDOW) - r
    bucket = _t5_bucket(rel)[..., None]
    table = rel_bias_table.astype(F32)
    bias = sum(jnp.where(bucket == b, table[b], 0.0) for b in range(N_BUCKETS)) * LOG2E
    bias = jnp.where((jnp.abs(rel) <= WINDOW)[..., None], bias, MASKED)
    bias = bias.transpose(2, 0, 1).reshape(N_KV_HEADS, GQA_GROUP, 3 * WINDOW, WINDOW)
    return bias.transpose(0, 2, 1, 3).reshape(N_KV_HEADS, 3 * WINDOW, GQA_GROUP * WINDOW)


def _trunk(x, a_w_qkv, a_q_gain, a_k_gain, a_w_o, b_w_qkv, b_sink, b_w_o, rel_bias_table,
           ln1_g, ln1_b, w_gate, w_up, w_down, ln2_g, ln2_b):
    B, S, _ = x.shape
    T = B * S
    tm_qkv = 512
    tm_tok = 512
    cos_t, sin_t = _rope_tables_t(S)
    for i in range(DEPTH):
        j = i // 2
        if i % 2 == 0:
            bq = 512
            gq = jnp.broadcast_to((a_q_gain[j] * Q_PRESCALE)[:, None], (HEAD_DIM, tm_qkv))
            gk = jnp.broadcast_to(a_k_gain[j][:, None], (HEAD_DIM, tm_qkv))
            qt, k, vt, qn2, kn2 = _qkv_project(x, a_w_qkv[j].T.astype(BF16), gq, gk, cos_t, sin_t,
                                               norm_rope=True, bq=bq, tm=tm_qkv)
            m = jnp.sqrt(qn2) * jnp.sqrt(jnp.max(kn2, axis=-1, keepdims=True))
            attn = functools.partial(_global_attention, bq=bq, bkc=512, n_split=4)
            attn_bounded = functools.partial(_global_attention_bounded, bq=bq, bkc=512, n_split=4)
            o = lax.cond(jnp.max(m) <= SAFE_LOG2_SPAN, lambda: attn_bounded(qt, m, k, vt),
                         lambda: attn(qt, k, vt))
            wo = a_w_o[j]
        else:
            ones = jnp.ones((HEAD_DIM, tm_qkv), F32)
            qt, k, vt = _qkv_project(x, b_w_qkv[j].T.astype(BF16), ones, ones, cos_t, sin_t,
                                     norm_rope=False, bq=WINDOW, tm=tm_qkv)
            bias_t = _window_bias_t(rel_bias_table)
            sink_t = jnp.repeat(b_sink[j].astype(F32) * LOG2E, WINDOW).reshape(
                N_KV_HEADS, 1, GQA_GROUP * WINDOW)
            o = _window_attention(qt, k, vt, bias_t, sink_t)
            wo = b_w_o[j]
        x2 = _out_proj_ln(x.reshape(T, D_MODEL), o.reshape(T, Q_DIM), wo.astype(BF16),
                          ln1_g[i][None], ln1_b[i][None], tm=tm_tok)
        x2 = _ffn_ln(x2, w_gate[i].astype(BF16), w_up[i].astype(BF16), w_down[i].astype(BF16),
                     ln2_g[i][None], ln2_b[i][None], tm=tm_tok)
        x = x2.reshape(B, S, D_MODEL)
    return x


def kernel(x_prompt, x_sample, a_w_qkv, a_q_gain, a_k_gain, a_w_o, b_w_qkv, b_sink, b_w_o,
           rel_bias_table, ln1_g, ln1_b, w_gate, w_up, w_down, ln2_g, ln2_b):
    weights = (a_w_qkv, a_q_gain, a_k_gain, a_w_o, b_w_qkv, b_sink, b_w_o, rel_bias_table,
               ln1_g, ln1_b, w_gate, w_up, w_down, ln2_g, ln2_b)
    return (_trunk(x_prompt, *weights), _trunk(x_sample, *weights))
```

```python
import functools
import math

import jax
import jax.numpy as jnp
from jax import lax
from jax.experimental import pallas as pl
from jax.experimental.pallas import tpu as pltpu

D_MODEL = 1024
DEPTH = 2
N_HEADS = 8
N_KV_HEADS = 2
HEAD_DIM = 128
GQA_GROUP = N_HEADS // N_KV_HEADS
Q_DIM = N_HEADS * HEAD_DIM
KV_DIM = N_KV_HEADS * HEAD_DIM
QKV_DIM = Q_DIM + 2 * KV_DIM
D_FF = 2816
GRID_W = 64
AXIS_DIM = HEAD_DIM // 2
ROPE_THETA = 10000.0
WINDOW = 128
N_BUCKETS = 32
MAX_DISTANCE = 128
LN_EPS = 1e-5
RMS_EPS = 1e-6
DEEPNORM_ALPHA = (2.0 * DEPTH) ** 0.25
LOG2E = math.log2(math.e)
Q_PRESCALE = HEAD_DIM ** -0.5 * LOG2E
MASKED = -1e30
SAFE_LOG2_SPAN = 48.0
BF16_ROUNDING_SLACK = 1.02

V7X_VMEM_LIMIT_BYTES = 56 * 1024 * 1024

BF16 = jnp.bfloat16
F32 = jnp.float32


def _params(sem):
    return pltpu.CompilerParams(dimension_semantics=sem, vmem_limit_bytes=V7X_VMEM_LIMIT_BYTES)


def _const_spec(shape):
    nd = len(shape)
    return pl.BlockSpec(shape, lambda *_: (0,) * nd, pipeline_mode=pl.Buffered(1))


HEADS_PER_DOT = 2


def _qkv_kernel(x_ref, wt_ref, *refs, norm_rope, bq):
    if norm_rope:
        cq_ref, sq_ref, ck_ref, sk_ref, qt_ref, k_ref, vt_ref = refs
    else:
        qt_ref, k_ref, vt_ref = refs
    tm = x_ref.shape[1]
    xb = x_ref[0].astype(BF16)

    def norm_rope_slab(y, cos_ref, sin_ref):
        r = lax.rsqrt(jnp.mean(y * y, axis=0, keepdims=True) + RMS_EPS)
        h = AXIS_DIM // 2
        partner = jnp.concatenate([y[h:2 * h], y[0:h], y[3 * h:4 * h], y[2 * h:3 * h]], axis=0)
        return (y * cos_ref[...] + partner * sin_ref[...]) * r

    rows = HEADS_PER_DOT * HEAD_DIM
    for grp in range(QKV_DIM // rows):
        yt = lax.dot_general(wt_ref[grp * rows:(grp + 1) * rows, :], xb, (((1,), (1,)), ((), ())),
                             preferred_element_type=F32)
        for sub in range(HEADS_PER_DOT):
            slab = yt[sub * HEAD_DIM:(sub + 1) * HEAD_DIM]
            head = grp * HEADS_PER_DOT + sub
            if head < N_HEADS:
                slab = norm_rope_slab(slab, cq_ref, sq_ref) if norm_rope else slab * Q_PRESCALE
                slab = slab.astype(BF16)
                kvh, g = divmod(head, GQA_GROUP)
                for j in range(tm // bq):
                    col = (j * GQA_GROUP + g) * bq
                    qt_ref[0, kvh, :, col:col + bq] = slab[:, j * bq:(j + 1) * bq]
            elif head < N_HEADS + N_KV_HEADS:
                kvh = head - N_HEADS
                if norm_rope:
                    slab = norm_rope_slab(slab, ck_ref, sk_ref)
                k_ref[0, :, kvh * HEAD_DIM:(kvh + 1) * HEAD_DIM] = slab.T.astype(BF16)
            else:
                kvh = head - N_HEADS - N_KV_HEADS
                vt_ref[0, kvh * HEAD_DIM:(kvh + 1) * HEAD_DIM, :] = slab.astype(BF16)


def _qkv_project(x, wt, tables, *, bq, tm):
    B, S, _ = x.shape
    kern = functools.partial(_qkv_kernel, norm_rope=bool(tables), bq=bq)
    return pl.pallas_call(
        kern,
        grid=(B, S // tm),
        in_specs=[
            pl.BlockSpec((1, tm, D_MODEL), lambda b, i: (b, i, 0)),
            _const_spec((QKV_DIM, D_MODEL)),
        ] + [pl.BlockSpec((HEAD_DIM, tm), lambda b, i: (0, i)) for _ in tables],
        out_specs=[
            pl.BlockSpec((1, N_KV_HEADS, HEAD_DIM, GQA_GROUP * tm), lambda b, i: (b, 0, 0, i)),
            pl.BlockSpec((1, tm, KV_DIM), lambda b, i: (b, i, 0)),
            pl.BlockSpec((1, KV_DIM, tm), lambda b, i: (b, 0, i)),
        ],
        out_shape=[
            jax.ShapeDtypeStruct((B, N_KV_HEADS, HEAD_DIM, GQA_GROUP * S), BF16),
            jax.ShapeDtypeStruct((B, S, KV_DIM), BF16),
            jax.ShapeDtypeStruct((B, KV_DIM, S), BF16),
        ],
        compiler_params=_params(("parallel", "parallel")),
        name="qkv_project",
    )(x, wt, *tables)


def _store_heads(o_ref, out_t, bq):
    for g in range(GQA_GROUP):
        o_ref[0, :, g * HEAD_DIM:(g + 1) * HEAD_DIM] = out_t[:, g * bq:(g + 1) * bq].T.astype(BF16)


def _global_attn_kernel(qt_ref, k_ref, vt_ref, o_ref, acc_ref, *, bq, bkc, n_split):
    S = k_ref.shape[1]
    nq = GQA_GROUP * bq
    w = nq // n_split
    acc_ref[...] = jnp.zeros_like(acc_ref)

    def body(c, carry):
        ms, ls = carry
        start = pl.multiple_of(c * bkc, bkc)
        k = k_ref[0, pl.ds(start, bkc), :]
        vt = vt_ref[0, :, pl.ds(start, bkc)]
        new_ms, new_ls = [], []
        for h in range(n_split):
            q = qt_ref[0, 0, :, h * w:(h + 1) * w]
            s = jnp.dot(k, q, preferred_element_type=F32)
            m_new = jnp.maximum(ms[h], jnp.max(s, axis=0, keepdims=True))
            alpha = jnp.exp2(ms[h] - m_new)
            p = jnp.exp2(s - m_new)
            new_ls.append(alpha * ls[h] + jnp.sum(p, axis=0, keepdims=True))
            new_ms.append(m_new)
            pv = jnp.dot(vt, p.astype(BF16), preferred_element_type=F32)
            acc_ref[:, h * w:(h + 1) * w] = alpha * acc_ref[:, h * w:(h + 1) * w] + pv
        return tuple(new_ms), tuple(new_ls)

    init = (tuple(jnp.full((1, w), MASKED, F32) for _ in range(n_split)),
            tuple(jnp.zeros((1, w), F32) for _ in range(n_split)))
    _, ls = lax.fori_loop(0, S // bkc, body, init)
    l = jnp.concatenate(ls, axis=1)
    _store_heads(o_ref, acc_ref[...] / l, bq)


def _global_attn_bounded_kernel(qt_ref, k_ref, vt_ref, o_ref, acc_ref, *, bq, bkc, n_split):
    S = k_ref.shape[1]
    nq = GQA_GROUP * bq
    w = nq // n_split
    acc_ref[...] = jnp.zeros_like(acc_ref)

    def body(c, ls):
        start = pl.multiple_of(c * bkc, bkc)
        k = k_ref[0, pl.ds(start, bkc), :]
        vt = vt_ref[0, :, pl.ds(start, bkc)]
        ss = [jnp.dot(k, qt_ref[0, 0, :, h * w:(h + 1) * w], preferred_element_type=F32)
              for h in range(n_split)]
        new_ls = []
        for h in range(n_split):
            cols = slice(h * w, (h + 1) * w)
            p = jnp.exp2(ss[h])
            new_ls.append(ls[h] + jnp.sum(p.reshape(bkc // 8, 8, w), axis=0))
            acc_ref[:, cols] += jnp.dot(vt, p.astype(BF16), preferred_element_type=F32)
        return tuple(new_ls)

    ls = lax.fori_loop(0, S // bkc, body, tuple(jnp.zeros((8, w), F32) for _ in range(n_split)),
                       unroll=2)
    l = jnp.concatenate([jnp.sum(x, axis=0, keepdims=True) for x in ls], axis=1)
    _store_heads(o_ref, acc_ref[...] / l, bq)


def _global_attention_bounded(qt, k, vt, *, bq, bkc, n_split):
    B, S, _ = k.shape
    nq = GQA_GROUP * bq
    kern = functools.partial(_global_attn_bounded_kernel, bq=bq, bkc=bkc, n_split=n_split)
    return pl.pallas_call(
        kern,
        grid=(B, N_KV_HEADS, S // bq),
        in_specs=[
            pl.BlockSpec((1, 1, HEAD_DIM, nq), lambda b, h, i: (b, h, 0, i)),
            pl.BlockSpec((1, S, HEAD_DIM), lambda b, h, i: (b, 0, h)),
            pl.BlockSpec((1, HEAD_DIM, S), lambda b, h, i: (b, h, 0)),
        ],
        out_specs=pl.BlockSpec((1, bq, GQA_GROUP * HEAD_DIM), lambda b, h, i: (b, i, h)),
        out_shape=jax.ShapeDtypeStruct((B, S, Q_DIM), BF16),
        scratch_shapes=[pltpu.VMEM((HEAD_DIM, nq), F32)],
        compiler_params=_params(("parallel", "parallel", "arbitrary")),
        name="global_attention_bounded",
    )(qt, k, vt)


def _global_attention(qt, k, vt, *, bq, bkc, n_split):
    B, S, _ = k.shape
    nq = GQA_GROUP * bq
    kern = functools.partial(_global_attn_kernel, bq=bq, bkc=bkc, n_split=n_split)
    return pl.pallas_call(
        kern,
        grid=(B, N_KV_HEADS, S // bq),
        in_specs=[
            pl.BlockSpec((1, 1, HEAD_DIM, nq), lambda b, h, i: (b, h, 0, i)),
            pl.BlockSpec((1, S, HEAD_DIM), lambda b, h, i: (b, 0, h)),
            pl.BlockSpec((1, HEAD_DIM, S), lambda b, h, i: (b, h, 0)),
        ],
        out_specs=pl.BlockSpec((1, bq, GQA_GROUP * HEAD_DIM), lambda b, h, i: (b, i, h)),
        out_shape=jax.ShapeDtypeStruct((B, S, Q_DIM), BF16),
        scratch_shapes=[pltpu.VMEM((HEAD_DIM, nq), F32)],
        compiler_params=_params(("parallel", "parallel", "arbitrary")),
        name="global_attention",
    )(qt, k, vt)


def _window_attn_kernel(qt_ref, k_ref, vt_ref, bias_ref, sink_ref, o_ref, *, qb):
    step = pl.program_id(2)
    last_step = pl.num_programs(2) - 1
    nb = k_ref.shape[1] // WINDOW
    nq = GQA_GROUP * WINDOW
    sink = sink_ref[0]

    def block_start(n):
        return pl.multiple_of(n * WINDOW, WINDOW)

    for qi in range(qb):
        n = step * qb + qi
        starts = [block_start(jnp.maximum(n - 1, 0)), block_start(n),
                  block_start(jnp.minimum(n + 1, nb - 1))]
        kw = jnp.concatenate([k_ref[0, pl.ds(st, WINDOW), :] for st in starts], axis=0)
        vw = jnp.concatenate([vt_ref[0, :, pl.ds(st, WINDOW)] for st in starts], axis=1)
        q = qt_ref[0, 0, :, qi * nq:(qi + 1) * nq]
        s = jnp.dot(kw, q, preferred_element_type=F32) + bias_ref[0]
        if qi == 0:
            s = jnp.concatenate(
                [jnp.where(step == 0, MASKED, s[:WINDOW]), s[WINDOW:]], axis=0)
        if qi == qb - 1:
            s = jnp.concatenate(
                [s[:2 * WINDOW], jnp.where(step == last_step, MASKED, s[2 * WINDOW:])], axis=0)
        m = jnp.maximum(jnp.max(s, axis=0, keepdims=True), sink)
        p = jnp.exp2(s - m)
        l = jnp.sum(p, axis=0, keepdims=True) + jnp.exp2(sink - m)
        out_t = jnp.dot(vw, p.astype(BF16), preferred_element_type=F32) / l
        for g in range(GQA_GROUP):
            o_ref[0, qi * WINDOW:(qi + 1) * WINDOW, g * HEAD_DIM:(g + 1) * HEAD_DIM] = (
                out_t[:, g * WINDOW:(g + 1) * WINDOW].T.astype(BF16))


def _window_attention(qt, k, vt, bias_t, sink_t, *, qb):
    B, S, _ = k.shape
    nq = GQA_GROUP * WINDOW
    return pl.pallas_call(
        functools.partial(_window_attn_kernel, qb=qb),
        grid=(B, N_KV_HEADS, S // (qb * WINDOW)),
        in_specs=[
            pl.BlockSpec((1, 1, HEAD_DIM, qb * nq), lambda b, h, n: (b, h, 0, n)),
            pl.BlockSpec((1, S, HEAD_DIM), lambda b, h, n: (b, 0, h)),
            pl.BlockSpec((1, HEAD_DIM, S), lambda b, h, n: (b, h, 0)),
            pl.BlockSpec((1, 3 * WINDOW, nq), lambda b, h, n: (h, 0, 0)),
            pl.BlockSpec((1, 1, nq), lambda b, h, n: (h, 0, 0)),
        ],
        out_specs=pl.BlockSpec((1, qb * WINDOW, GQA_GROUP * HEAD_DIM), lambda b, h, n: (b, n, h)),
        out_shape=jax.ShapeDtypeStruct((B, S, Q_DIM), BF16),
        compiler_params=_params(("parallel", "parallel", "parallel")),
        name="window_attention",
    )(qt, k, vt, bias_t, sink_t)


def _layer_norm(y, g, b):
    mu = jnp.mean(y, axis=-1, keepdims=True)
    d = y - mu
    var = jnp.mean(d * d, axis=-1, keepdims=True)
    return d * lax.rsqrt(var + LN_EPS) * g + b


def _post_attention_kernel(x_ref, o_ref, wo_ref, g1_ref, b1_ref, wg_ref, wu_ref, wd_ref, g2_ref,
                           b2_ref, y_ref):
    h = jnp.dot(o_ref[...], wo_ref[...], preferred_element_type=F32)
    x1 = _layer_norm(DEEPNORM_ALPHA * x_ref[...] + h, g1_ref[...], b1_ref[...])
    xb = x1.astype(BF16)
    gate = jnp.dot(xb, wg_ref[...], preferred_element_type=F32)
    up = jnp.dot(xb, wu_ref[...], preferred_element_type=F32)
    mid = (gate * jax.nn.sigmoid(gate) * up).astype(BF16)
    h = jnp.dot(mid, wd_ref[...], preferred_element_type=F32)
    y_ref[...] = _layer_norm(DEEPNORM_ALPHA * x1 + h, g2_ref[...], b2_ref[...])


def _post_attention(x, o, wo, g1, b1, wg, wu, wd, g2, b2, *, tm):
    T = x.shape[0]
    vec = _const_spec((1, D_MODEL))
    return pl.pallas_call(
        _post_attention_kernel,
        grid=(T // tm,),
        in_specs=[
            pl.BlockSpec((tm, D_MODEL), lambda i: (i, 0)),
            pl.BlockSpec((tm, Q_DIM), lambda i: (i, 0)),
            _const_spec((Q_DIM, D_MODEL)), vec, vec,
            _const_spec((D_MODEL, D_FF)),
            _const_spec((D_MODEL, D_FF)),
            _const_spec((D_FF, D_MODEL)), vec, vec,
        ],
        out_specs=pl.BlockSpec((tm, D_MODEL), lambda i: (i, 0)),
        out_shape=jax.ShapeDtypeStruct((T, D_MODEL), F32),
        compiler_params=_params(("parallel",)),
        name="post_attention",
    )(x, o, wo, g1, b1, wg, wu, wd, g2, b2)


def _rope_tables_t(seq_len):
    rows_n = seq_len // GRID_W
    rows = jnp.repeat(jnp.arange(rows_n, dtype=F32), GRID_W)
    cols = jnp.tile(jnp.arange(GRID_W, dtype=F32), rows_n)
    inv_freq = ROPE_THETA ** (-jnp.arange(0, AXIS_DIM, 2, dtype=F32) / AXIS_DIM)
    ang_r = (rows[:, None] * inv_freq).T
    ang_c = (cols[:, None] * inv_freq).T
    cos_t = jnp.concatenate([jnp.cos(ang_r)] * 2 + [jnp.cos(ang_c)] * 2, axis=0)
    sin_t = jnp.concatenate([-jnp.sin(ang_r), jnp.sin(ang_r), -jnp.sin(ang_c), jnp.sin(ang_c)],
                            axis=0)
    return cos_t, sin_t


def _rope_partner(g):
    h = AXIS_DIM // 2
    return jnp.concatenate([g[h:2 * h], g[0:h], g[3 * h:4 * h], g[2 * h:3 * h]])


def _t5_bucket(rel):
    nb = N_BUCKETS // 2
    max_exact = nb // 2
    base = (rel > 0).astype(jnp.int32) * nb
    n = jnp.abs(rel)
    nf = jnp.maximum(n, max_exact).astype(F32)
    large = max_exact + (jnp.log(nf / max_exact) / math.log(MAX_DISTANCE / max_exact)
                         * (nb - max_exact)).astype(jnp.int32)
    large = jnp.minimum(large, nb - 1)
    return base + jnp.where(n < max_exact, n, large)


def _window_bias_t(rel_bias_table):
    r = jnp.arange(WINDOW)[None, :]
    c = jnp.arange(3 * WINDOW)[:, None]
    rel = (c - WINDOW) - r
    bucket = _t5_bucket(rel)[..., None]
    table = rel_bias_table.astype(F32)
    bias = sum(jnp.where(bucket == b, table[b], 0.0) for b in range(N_BUCKETS)) * LOG2E
    bias = jnp.where((jnp.abs(rel) <= WINDOW)[..., None], bias, MASKED)
    bias = bias.transpose(2, 0, 1).reshape(N_KV_HEADS, GQA_GROUP, 3 * WINDOW, WINDOW)
    return bias.transpose(0, 2, 1, 3).reshape(N_KV_HEADS, 3 * WINDOW, GQA_GROUP * WINDOW)


def _trunk(x, a_w_qkv, a_q_gain, a_k_gain, a_w_o, b_w_qkv, b_sink, b_w_o, rel_bias_table,
           ln1_g, ln1_b, w_gate, w_up, w_down, ln2_g, ln2_b):
    B, S, _ = x.shape
    T = B * S
    tm_qkv = 512
    tm_tok = 512
    cos_t, sin_t = _rope_tables_t(S)
    for i in range(DEPTH):
        j = i // 2
        if i % 2 == 0:
            bq = 512
            gq = a_q_gain[j].astype(F32) * Q_PRESCALE
            gk = a_k_gain[j].astype(F32)
            tables = (gq[:, None] * cos_t, _rope_partner(gq)[:, None] * sin_t,
                      gk[:, None] * cos_t, _rope_partner(gk)[:, None] * sin_t)
            qt, k, vt = _qkv_project(x, a_w_qkv[j].T.astype(BF16), tables, bq=bq, tm=tm_qkv)
            bound = HEAD_DIM * jnp.max(jnp.abs(gq)) * jnp.max(jnp.abs(gk)) * BF16_ROUNDING_SLACK
            attn = functools.partial(_global_attention, bq=bq, bkc=512, n_split=4)
            attn_bounded = functools.partial(_global_attention_bounded, bq=bq, bkc=512, n_split=4)
            o = lax.cond(bound <= SAFE_LOG2_SPAN, attn_bounded, attn, qt, k, vt)
            wo = a_w_o[j]
        else:
            qt, k, vt = _qkv_project(x, b_w_qkv[j].T.astype(BF16), (), bq=WINDOW, tm=tm_qkv)
            bias_t = _window_bias_t(rel_bias_table)
            sink_t = jnp.repeat(b_sink[j].astype(F32) * LOG2E, WINDOW).reshape(
                N_KV_HEADS, 1, GQA_GROUP * WINDOW)
            o = _window_attention(qt, k, vt, bias_t, sink_t, qb=4)
            wo = b_w_o[j]
        x2 = _post_attention(x.reshape(T, D_MODEL), o.reshape(T, Q_DIM), wo.astype(BF16),
                             ln1_g[i][None], ln1_b[i][None], w_gate[i].astype(BF16),
                             w_up[i].astype(BF16), w_down[i].astype(BF16),
                             ln2_g[i][None], ln2_b[i][None], tm=tm_tok)
        x = x2.reshape(B, S, D_MODEL)
    return x


def kernel(x_prompt, x_sample, a_w_qkv, a_q_gain, a_k_gain, a_w_o, b_w_qkv, b_sink, b_w_o,
           rel_bias_table, ln1_g, ln1_b, w_gate, w_up, w_down, ln2_g, ln2_b):
    weights = (a_w_qkv, a_q_gain, a_k_gain, a_w_o, b_w_qkv, b_sink, b_w_o, rel_bias_table,
               ln1_g, ln1_b, w_gate, w_up, w_down, ln2_g, ln2_b)
    return (_trunk(x_prompt, *weights), _trunk(x_sample, *weights))
```

```python
import functools
import math

import jax
import jax.numpy as jnp
from jax import lax
from jax.experimental import pallas as pl
from jax.experimental.pallas import tpu as pltpu

D_MODEL = 1024
DEPTH = 2
N_HEADS = 8
N_KV_HEADS = 2
HEAD_DIM = 128
GQA_GROUP = N_HEADS // N_KV_HEADS
Q_DIM = N_HEADS * HEAD_DIM
KV_DIM = N_KV_HEADS * HEAD_DIM
QKV_DIM = Q_DIM + 2 * KV_DIM
D_FF = 2816
GRID_W = 64
AXIS_DIM = HEAD_DIM // 2
ROPE_THETA = 10000.0
WINDOW = 128
N_BUCKETS = 32
MAX_DISTANCE = 128
LN_EPS = 1e-5
RMS_EPS = 1e-6
DEEPNORM_ALPHA = (2.0 * DEPTH) ** 0.25
LOG2E = math.log2(math.e)
Q_PRESCALE = HEAD_DIM ** -0.5 * LOG2E
MASKED = -1e30
SAFE_LOG2_SPAN = 48.0
BF16_ROUNDING_SLACK = 1.02

V7X_VMEM_LIMIT_BYTES = 56 * 1024 * 1024

BF16 = jnp.bfloat16
F32 = jnp.float32


def _params(sem):
    return pltpu.CompilerParams(dimension_semantics=sem, vmem_limit_bytes=V7X_VMEM_LIMIT_BYTES)


def _const_spec(shape):
    nd = len(shape)
    return pl.BlockSpec(shape, lambda *_: (0,) * nd, pipeline_mode=pl.Buffered(1))


HEADS_PER_DOT = 2


def _qkv_kernel(x_ref, wt_ref, *refs, norm_rope, bq):
    if norm_rope:
        cq_ref, sq_ref, ck_ref, sk_ref, qt_ref, k_ref, vt_ref = refs
    else:
        qt_ref, k_ref, vt_ref = refs
    tm = x_ref.shape[1]
    xb = x_ref[0].astype(BF16)

    def norm_rope_slab(y, cos_ref, sin_ref):
        r = lax.rsqrt(jnp.mean(y * y, axis=0, keepdims=True) + RMS_EPS)
        h = AXIS_DIM // 2
        partner = jnp.concatenate([y[h:2 * h], y[0:h], y[3 * h:4 * h], y[2 * h:3 * h]], axis=0)
        return (y * cos_ref[...] + partner * sin_ref[...]) * r

    rows = HEADS_PER_DOT * HEAD_DIM
    for grp in range(QKV_DIM // rows):
        yt = lax.dot_general(wt_ref[grp * rows:(grp + 1) * rows, :], xb, (((1,), (1,)), ((), ())),
                             preferred_element_type=F32)
        for sub in range(HEADS_PER_DOT):
            slab = yt[sub * HEAD_DIM:(sub + 1) * HEAD_DIM]
            head = grp * HEADS_PER_DOT + sub
            if head < N_HEADS:
                slab = norm_rope_slab(slab, cq_ref, sq_ref) if norm_rope else slab * Q_PRESCALE
                slab = slab.astype(BF16)
                kvh, g = divmod(head, GQA_GROUP)
                for j in range(tm // bq):
                    col = (j * GQA_GROUP + g) * bq
                    qt_ref[0, kvh, :, col:col + bq] = slab[:, j * bq:(j + 1) * bq]
            elif head < N_HEADS + N_KV_HEADS:
                kvh = head - N_HEADS
                if norm_rope:
                    slab = norm_rope_slab(slab, ck_ref, sk_ref)
                k_ref[0, :, kvh * HEAD_DIM:(kvh + 1) * HEAD_DIM] = slab.T.astype(BF16)
            else:
                kvh = head - N_HEADS - N_KV_HEADS
                vt_ref[0, kvh * HEAD_DIM:(kvh + 1) * HEAD_DIM, :] = slab.astype(BF16)


def _qkv_project(x, wt, tables, *, bq, tm):
    B, S, _ = x.shape
    kern = functools.partial(_qkv_kernel, norm_rope=bool(tables), bq=bq)
    return pl.pallas_call(
        kern,
        grid=(B, S // tm),
        in_specs=[
            pl.BlockSpec((1, tm, D_MODEL), lambda b, i: (b, i, 0)),
            _const_spec((QKV_DIM, D_MODEL)),
        ] + [pl.BlockSpec((HEAD_DIM, tm), lambda b, i: (0, i)) for _ in tables],
        out_specs=[
            pl.BlockSpec((1, N_KV_HEADS, HEAD_DIM, GQA_GROUP * tm), lambda b, i: (b, 0, 0, i)),
            pl.BlockSpec((1, tm, KV_DIM), lambda b, i: (b, i, 0)),
            pl.BlockSpec((1, KV_DIM, tm), lambda b, i: (b, 0, i)),
        ],
        out_shape=[
            jax.ShapeDtypeStruct((B, N_KV_HEADS, HEAD_DIM, GQA_GROUP * S), BF16),
            jax.ShapeDtypeStruct((B, S, KV_DIM), BF16),
            jax.ShapeDtypeStruct((B, KV_DIM, S), BF16),
        ],
        compiler_params=_params(("parallel", "parallel")),
        name="qkv_project",
    )(x, wt, *tables)


def _store_heads(o_ref, out_t, bq):
    for g in range(GQA_GROUP):
        o_ref[0, :, g * HEAD_DIM:(g + 1) * HEAD_DIM] = out_t[:, g * bq:(g + 1) * bq].T.astype(BF16)


def _global_attn_kernel(qt_ref, k_ref, vt_ref, o_ref, acc_ref, *, bq, bkc, n_split):
    S = k_ref.shape[1]
    nq = GQA_GROUP * bq
    w = nq // n_split
    acc_ref[...] = jnp.zeros_like(acc_ref)

    def body(c, carry):
        ms, ls = carry
        start = pl.multiple_of(c * bkc, bkc)
        k = k_ref[0, pl.ds(start, bkc), :]
        vt = vt_ref[0, :, pl.ds(start, bkc)]
        new_ms, new_ls = [], []
        for h in range(n_split):
            q = qt_ref[0, 0, :, h * w:(h + 1) * w]
            s = jnp.dot(k, q, preferred_element_type=F32)
            m_new = jnp.maximum(ms[h], jnp.max(s, axis=0, keepdims=True))
            alpha = jnp.exp2(ms[h] - m_new)
            p = jnp.exp2(s - m_new)
            new_ls.append(alpha * ls[h] + jnp.sum(p, axis=0, keepdims=True))
            new_ms.append(m_new)
            pv = jnp.dot(vt, p.astype(BF16), preferred_element_type=F32)
            acc_ref[:, h * w:(h + 1) * w] = alpha * acc_ref[:, h * w:(h + 1) * w] + pv
        return tuple(new_ms), tuple(new_ls)

    init = (tuple(jnp.full((1, w), MASKED, F32) for _ in range(n_split)),
            tuple(jnp.zeros((1, w), F32) for _ in range(n_split)))
    _, ls = lax.fori_loop(0, S // bkc, body, init)
    l = jnp.concatenate(ls, axis=1)
    _store_heads(o_ref, acc_ref[...] / l, bq)


def _global_attn_bounded_kernel(qt_ref, k_ref, vt_ref, o_ref, acc_ref, *, bq, bkc, n_split):
    S = k_ref.shape[1]
    nq = GQA_GROUP * bq
    w = nq // n_split
    acc_ref[...] = jnp.zeros_like(acc_ref)

    def body(c, ls):
        start = pl.multiple_of(c * bkc, bkc)
        k = k_ref[0, pl.ds(start, bkc), :]
        vt = vt_ref[0, :, pl.ds(start, bkc)]
        ss = [jnp.dot(k, qt_ref[0, 0, :, h * w:(h + 1) * w], preferred_element_type=F32)
              for h in range(n_split)]
        new_ls = []
        for h in range(n_split):
            cols = slice(h * w, (h + 1) * w)
            p = jnp.exp2(ss[h])
            new_ls.append(ls[h] + jnp.sum(p.reshape(bkc // 8, 8, w), axis=0))
            acc_ref[:, cols] += jnp.dot(vt, p.astype(BF16), preferred_element_type=F32)
        return tuple(new_ls)

    ls = lax.fori_loop(0, S // bkc, body, tuple(jnp.zeros((8, w), F32) for _ in range(n_split)),
                       unroll=2)
    l = jnp.concatenate([jnp.sum(x, axis=0, keepdims=True) for x in ls], axis=1)
    _store_heads(o_ref, acc_ref[...] / l, bq)


def _global_attention_bounded(qt, k, vt, *, bq, bkc, n_split):
    B, S, _ = k.shape
    nq = GQA_GROUP * bq
    kern = functools.partial(_global_attn_bounded_kernel, bq=bq, bkc=bkc, n_split=n_split)
    return pl.pallas_call(
        kern,
        grid=(B, N_KV_HEADS, S // bq),
        in_specs=[
            pl.BlockSpec((1, 1, HEAD_DIM, nq), lambda b, h, i: (b, h, 0, i)),
            pl.BlockSpec((1, S, HEAD_DIM), lambda b, h, i: (b, 0, h)),
            pl.BlockSpec((1, HEAD_DIM, S), lambda b, h, i: (b, h, 0)),
        ],
        out_specs=pl.BlockSpec((1, bq, GQA_GROUP * HEAD_DIM), lambda b, h, i: (b, i, h)),
        out_shape=jax.ShapeDtypeStruct((B, S, Q_DIM), BF16),
        scratch_shapes=[pltpu.VMEM((HEAD_DIM, nq), F32)],
        compiler_params=_params(("parallel", "parallel", "arbitrary")),
        name="global_attention_bounded",
    )(qt, k, vt)


def _global_attention(qt, k, vt, *, bq, bkc, n_split):
    B, S, _ = k.shape
    nq = GQA_GROUP * bq
    kern = functools.partial(_global_attn_kernel, bq=bq, bkc=bkc, n_split=n_split)
    return pl.pallas_call(
        kern,
        grid=(B, N_KV_HEADS, S // bq),
        in_specs=[
            pl.BlockSpec((1, 1, HEAD_DIM, nq), lambda b, h, i: (b, h, 0, i)),
            pl.BlockSpec((1, S, HEAD_DIM), lambda b, h, i: (b, 0, h)),
            pl.BlockSpec((1, HEAD_DIM, S), lambda b, h, i: (b, h, 0)),
        ],
        out_specs=pl.BlockSpec((1, bq, GQA_GROUP * HEAD_DIM), lambda b, h, i: (b, i, h)),
        out_shape=jax.ShapeDtypeStruct((B, S, Q_DIM), BF16),
        scratch_shapes=[pltpu.VMEM((HEAD_DIM, nq), F32)],
        compiler_params=_params(("parallel", "parallel", "arbitrary")),
        name="global_attention",
    )(qt, k, vt)


def _window_attn_kernel(qt_ref, k_ref, vt_ref, bias_ref, sink_ref, o_ref, s_ref, *, qb):
    step = pl.program_id(2)
    last_step = pl.num_programs(2) - 1
    nb = k_ref.shape[1] // WINDOW
    nq = GQA_GROUP * WINDOW
    sink = sink_ref[0]

    def block_start(n):
        return pl.multiple_of(n * WINDOW, WINDOW)

    starts = [[block_start(jnp.maximum(step * qb + qi - 1, 0)), block_start(step * qb + qi),
               block_start(jnp.minimum(step * qb + qi + 1, nb - 1))] for qi in range(qb)]
    for qi in range(qb):
        kw = jnp.concatenate([k_ref[0, pl.ds(st, WINDOW), :] for st in starts[qi]], axis=0)
        s_ref[qi] = jnp.dot(kw, qt_ref[0, 0, :, qi * nq:(qi + 1) * nq],
                            preferred_element_type=F32)
    for qi in range(qb):
        vw = jnp.concatenate([vt_ref[0, :, pl.ds(st, WINDOW)] for st in starts[qi]], axis=1)
        s = s_ref[qi] + bias_ref[0]
        if qi == 0:
            s = jnp.concatenate(
                [jnp.where(step == 0, MASKED, s[:WINDOW]), s[WINDOW:]], axis=0)
        if qi == qb - 1:
            s = jnp.concatenate(
                [s[:2 * WINDOW], jnp.where(step == last_step, MASKED, s[2 * WINDOW:])], axis=0)
        m = jnp.maximum(jnp.max(s, axis=0, keepdims=True), sink)
        p = jnp.exp2(s - m)
        l = jnp.sum(p, axis=0, keepdims=True) + jnp.exp2(sink - m)
        out_t = jnp.dot(vw, p.astype(BF16), preferred_element_type=F32) / l
        for g in range(GQA_GROUP):
            o_ref[0, qi * WINDOW:(qi + 1) * WINDOW, g * HEAD_DIM:(g + 1) * HEAD_DIM] = (
                out_t[:, g * WINDOW:(g + 1) * WINDOW].T.astype(BF16))


def _window_attention(qt, k, vt, bias_t, sink_t, *, qb):
    B, S, _ = k.shape
    nq = GQA_GROUP * WINDOW
    return pl.pallas_call(
        functools.partial(_window_attn_kernel, qb=qb),
        grid=(B, N_KV_HEADS, S // (qb * WINDOW)),
        in_specs=[
            pl.BlockSpec((1, 1, HEAD_DIM, qb * nq), lambda b, h, n: (b, h, 0, n)),
            pl.BlockSpec((1, S, HEAD_DIM), lambda b, h, n: (b, 0, h)),
            pl.BlockSpec((1, HEAD_DIM, S), lambda b, h, n: (b, h, 0)),
            pl.BlockSpec((1, 3 * WINDOW, nq), lambda b, h, n: (h, 0, 0)),
            pl.BlockSpec((1, 1, nq), lambda b, h, n: (h, 0, 0)),
        ],
        out_specs=pl.BlockSpec((1, qb * WINDOW, GQA_GROUP * HEAD_DIM), lambda b, h, n: (b, n, h)),
        out_shape=jax.ShapeDtypeStruct((B, S, Q_DIM), BF16),
        scratch_shapes=[pltpu.VMEM((qb, 3 * WINDOW, nq), F32)],
        compiler_params=_params(("parallel", "parallel", "parallel")),
        name="window_attention",
    )(qt, k, vt, bias_t, sink_t)


def _layer_norm(y, g, b):
    mu = jnp.mean(y, axis=-1, keepdims=True)
    d = y - mu
    var = jnp.mean(d * d, axis=-1, keepdims=True)
    return d * lax.rsqrt(var + LN_EPS) * g + b


ROW_GROUPS = 2


def _post_attention_kernel(x_ref, o_ref, wo_ref, g1_ref, b1_ref, wg_ref, wu_ref, wd_ref, g2_ref,
                           b2_ref, y_ref):
    tm = x_ref.shape[0]
    rows = [slice(r * tm // ROW_GROUPS, (r + 1) * tm // ROW_GROUPS) for r in range(ROW_GROUPS)]
    hs = [jnp.dot(o_ref[r, :], wo_ref[...], preferred_element_type=F32) for r in rows]
    x1s = [_layer_norm(DEEPNORM_ALPHA * x_ref[r, :] + h, g1_ref[...], b1_ref[...])
           for r, h in zip(rows, hs)]
    mids = []
    for x1 in x1s:
        xb = x1.astype(BF16)
        gate = jnp.dot(xb, wg_ref[...], preferred_element_type=F32)
        up = jnp.dot(xb, wu_ref[...], preferred_element_type=F32)
        mids.append((gate * jax.nn.sigmoid(gate) * up).astype(BF16))
    hs = [jnp.dot(mid, wd_ref[...], preferred_element_type=F32) for mid in mids]
    for r, x1, h in zip(rows, x1s, hs):
        y_ref[r, :] = _layer_norm(DEEPNORM_ALPHA * x1 + h, g2_ref[...], b2_ref[...])


def _post_attention(x, o, wo, g1, b1, wg, wu, wd, g2, b2, *, tm):
    T = x.shape[0]
    vec = _const_spec((1, D_MODEL))
    return pl.pallas_call(
        _post_attention_kernel,
        grid=(T // tm,),
        in_specs=[
            pl.BlockSpec((tm, D_MODEL), lambda i: (i, 0)),
            pl.BlockSpec((tm, Q_DIM), lambda i: (i, 0)),
            _const_spec((Q_DIM, D_MODEL)), vec, vec,
            _const_spec((D_MODEL, D_FF)),
            _const_spec((D_MODEL, D_FF)),
            _const_spec((D_FF, D_MODEL)), vec, vec,
        ],
        out_specs=pl.BlockSpec((tm, D_MODEL), lambda i: (i, 0)),
        out_shape=jax.ShapeDtypeStruct((T, D_MODEL), F32),
        compiler_params=_params(("parallel",)),
        name="post_attention",
    )(x, o, wo, g1, b1, wg, wu, wd, g2, b2)


def _rope_tables_t(seq_len):
    rows_n = seq_len // GRID_W
    rows = jnp.repeat(jnp.arange(rows_n, dtype=F32), GRID_W)
    cols = jnp.tile(jnp.arange(GRID_W, dtype=F32), rows_n)
    inv_freq = ROPE_THETA ** (-jnp.arange(0, AXIS_DIM, 2, dtype=F32) / AXIS_DIM)
    ang_r = (rows[:, None] * inv_freq).T
    ang_c = (cols[:, None] * inv_freq).T
    cos_t = jnp.concatenate([jnp.cos(ang_r)] * 2 + [jnp.cos(ang_c)] * 2, axis=0)
    sin_t = jnp.concatenate([-jnp.sin(ang_r), jnp.sin(ang_r), -jnp.sin(ang_c), jnp.sin(ang_c)],
                            axis=0)
    return cos_t, sin_t


def _rope_partner(g):
    h = AXIS_DIM // 2
    return jnp.concatenate([g[h:2 * h], g[0:h], g[3 * h:4 * h], g[2 * h:3 * h]])


def _t5_bucket(rel):
    nb = N_BUCKETS // 2
    max_exact = nb // 2
    base = (rel > 0).astype(jnp.int32) * nb
    n = jnp.abs(rel)
    nf = jnp.maximum(n, max_exact).astype(F32)
    large = max_exact + (jnp.log(nf / max_exact) / math.log(MAX_DISTANCE / max_exact)
                         * (nb - max_exact)).astype(jnp.int32)
    large = jnp.minimum(large, nb - 1)
    return base + jnp.where(n < max_exact, n, large)


def _window_bias_t(rel_bias_table):
    r = jnp.arange(WINDOW)[None, :]
    c = jnp.arange(3 * WINDOW)[:, None]
    rel = (c - WINDOW) - r
    bucket = _t5_bucket(rel)[..., None]
    table = rel_bias_table.astype(F32)
    bias = sum(jnp.where(bucket == b, table[b], 0.0) for b in range(N_BUCKETS)) * LOG2E
    bias = jnp.where((jnp.abs(rel) <= WINDOW)[..., None], bias, MASKED)
    bias = bias.transpose(2, 0, 1).reshape(N_KV_HEADS, GQA_GROUP, 3 * WINDOW, WINDOW)
    return bias.transpose(0, 2, 1, 3).reshape(N_KV_HEADS, 3 * WINDOW, GQA_GROUP * WINDOW)


def _trunk(x, a_w_qkv, a_q_gain, a_k_gain, a_w_o, b_w_qkv, b_sink, b_w_o, rel_bias_table,
           ln1_g, ln1_b, w_gate, w_up, w_down, ln2_g, ln2_b):
    B, S, _ = x.shape
    T = B * S
    tm_qkv = 512
    tm_tok = 512
    cos_t, sin_t = _rope_tables_t(S)
    for i in range(DEPTH):
        j = i // 2
        if i % 2 == 0:
            bq = 512
            gq = a_q_gain[j].astype(F32) * Q_PRESCALE
            gk = a_k_gain[j].astype(F32)
            tables = (gq[:, None] * cos_t, _rope_partner(gq)[:, None] * sin_t,
                      gk[:, None] * cos_t, _rope_partner(gk)[:, None] * sin_t)
            qt, k, vt = _qkv_project(x, a_w_qkv[j].T.astype(BF16), tables, bq=bq, tm=tm_qkv)
            bound = HEAD_DIM * jnp.max(jnp.abs(gq)) * jnp.max(jnp.abs(gk)) * BF16_ROUNDING_SLACK
            attn = functools.partial(_global_attention, bq=bq, bkc=512, n_split=4)
            attn_bounded = functools.partial(_global_attention_bounded, bq=bq, bkc=512, n_split=4)
            o = lax.cond(bound <= SAFE_LOG2_SPAN, attn_bounded, attn, qt, k, vt)
            wo = a_w_o[j]
        else:
            qt, k, vt = _qkv_project(x, b_w_qkv[j].T.astype(BF16), (), bq=WINDOW, tm=tm_qkv)
            bias_t = _window_bias_t(rel_bias_table)
            sink_t = jnp.repeat(b_sink[j].astype(F32) * LOG2E, WINDOW).reshape(
                N_KV_HEADS, 1, GQA_GROUP * WINDOW)
            o = _window_attention(qt, k, vt, bias_t, sink_t, qb=4)
            wo = b_w_o[j]
        x2 = _post_attention(x.reshape(T, D_MODEL), o.reshape(T, Q_DIM), wo.astype(BF16),
                             ln1_g[i][None], ln1_b[i][None], w_gate[i].astype(BF16),
                             w_up[i].astype(BF16), w_down[i].astype(BF16),
                             ln2_g[i][None], ln2_b[i][None], tm=tm_tok)
        x = x2.reshape(B, S, D_MODEL)
    return x


def kernel(x_prompt, x_sample, a_w_qkv, a_q_gain, a_k_gain, a_w_o, b_w_qkv, b_sink, b_w_o,
           rel_bias_table, ln1_g, ln1_b, w_gate, w_up, w_down, ln2_g, ln2_b):
    weights = (a_w_qkv, a_q_gain, a_k_gain, a_w_o, b_w_qkv, b_sink, b_w_o, rel_bias_table,
               ln1_g, ln1_b, w_gate, w_up, w_down, ln2_g, ln2_b)
    return (_trunk(x_prompt, *weights), _trunk(x_sample, *weights))
```

```python
import functools
import math

import jax
import jax.numpy as jnp
from jax import lax
from jax.experimental import pallas as pl
from jax.experimental.pallas import tpu as pltpu

D_MODEL = 1024
DEPTH = 2
N_HEADS = 8
N_KV_HEADS = 2
HEAD_DIM = 128
GQA_GROUP = N_HEADS // N_KV_HEADS
Q_DIM = N_HEADS * HEAD_DIM
KV_DIM = N_KV_HEADS * HEAD_DIM
QKV_DIM = Q_DIM + 2 * KV_DIM
D_FF = 2816
GRID_W = 64
AXIS_DIM = HEAD_DIM // 2
ROPE_THETA = 10000.0
WINDOW = 128
N_BUCKETS = 32
MAX_DISTANCE = 128
LN_EPS = 1e-5
RMS_EPS = 1e-6
DEEPNORM_ALPHA = (2.0 * DEPTH) ** 0.25
LOG2E = math.log2(math.e)
Q_PRESCALE = HEAD_DIM ** -0.5 * LOG2E
MASKED = -1e30
SAFE_LOG2_SPAN = 48.0
BF16_ROUNDING_SLACK = 1.02

V7X_VMEM_LIMIT_BYTES = 56 * 1024 * 1024

BF16 = jnp.bfloat16
F32 = jnp.float32


def _params(sem):
    return pltpu.CompilerParams(dimension_semantics=sem, vmem_limit_bytes=V7X_VMEM_LIMIT_BYTES)


def _const_spec(shape):
    nd = len(shape)
    return pl.BlockSpec(shape, lambda *_: (0,) * nd, pipeline_mode=pl.Buffered(1))


HEADS_PER_DOT = 2


def _qkv_kernel(x_ref, wt_ref, *refs, norm_rope, bq):
    if norm_rope:
        cq_ref, sq_ref, ck_ref, sk_ref, qt_ref, k_ref, vt_ref = refs
    else:
        qt_ref, k_ref, vt_ref = refs
    tm = x_ref.shape[1]
    xb = x_ref[0].astype(BF16)

    def norm_rope_slab(y, cos_ref, sin_ref):
        r = lax.rsqrt(jnp.mean(y * y, axis=0, keepdims=True) + RMS_EPS)
        h = AXIS_DIM // 2
        partner = jnp.concatenate([y[h:2 * h], y[0:h], y[3 * h:4 * h], y[2 * h:3 * h]], axis=0)
        return (y * cos_ref[...] + partner * sin_ref[...]) * r

    rows = HEADS_PER_DOT * HEAD_DIM
    for grp in range(QKV_DIM // rows):
        yt = lax.dot_general(wt_ref[grp * rows:(grp + 1) * rows, :], xb, (((1,), (1,)), ((), ())),
                             preferred_element_type=F32)
        for sub in range(HEADS_PER_DOT):
            slab = yt[sub * HEAD_DIM:(sub + 1) * HEAD_DIM]
            head = grp * HEADS_PER_DOT + sub
            if head < N_HEADS:
                slab = norm_rope_slab(slab, cq_ref, sq_ref) if norm_rope else slab * Q_PRESCALE
                slab = slab.astype(BF16)
                kvh, g = divmod(head, GQA_GROUP)
                for j in range(tm // bq):
                    col = (j * GQA_GROUP + g) * bq
                    qt_ref[0, kvh, :, col:col + bq] = slab[:, j * bq:(j + 1) * bq]
            elif head < N_HEADS + N_KV_HEADS:
                kvh = head - N_HEADS
                if norm_rope:
                    slab = norm_rope_slab(slab, ck_ref, sk_ref)
                k_ref[0, kvh] = slab.T.astype(BF16)
            else:
                kvh = head - N_HEADS - N_KV_HEADS
                vt_ref[0, kvh * HEAD_DIM:(kvh + 1) * HEAD_DIM, :] = slab.astype(BF16)


def _qkv_project(x, wt, tables, *, bq, tm):
    B, S, _ = x.shape
    kern = functools.partial(_qkv_kernel, norm_rope=bool(tables), bq=bq)
    return pl.pallas_call(
        kern,
        grid=(B, S // tm),
        in_specs=[
            pl.BlockSpec((1, tm, D_MODEL), lambda b, i: (b, i, 0)),
            _const_spec((QKV_DIM, D_MODEL)),
        ] + [pl.BlockSpec((HEAD_DIM, tm), lambda b, i: (0, i)) for _ in tables],
        out_specs=[
            pl.BlockSpec((1, N_KV_HEADS, HEAD_DIM, GQA_GROUP * tm), lambda b, i: (b, 0, 0, i)),
            pl.BlockSpec((1, N_KV_HEADS, tm, HEAD_DIM), lambda b, i: (b, 0, i, 0)),
            pl.BlockSpec((1, KV_DIM, tm), lambda b, i: (b, 0, i)),
        ],
        out_shape=[
            jax.ShapeDtypeStruct((B, N_KV_HEADS, HEAD_DIM, GQA_GROUP * S), BF16),
            jax.ShapeDtypeStruct((B, N_KV_HEADS, S, HEAD_DIM), BF16),
            jax.ShapeDtypeStruct((B, KV_DIM, S), BF16),
        ],
        compiler_params=_params(("parallel", "parallel")),
        name="qkv_project",
    )(x, wt, *tables)


def _store_heads(o_ref, out_t, bq):
    for g in range(GQA_GROUP):
        o_ref[0, :, g * HEAD_DIM:(g + 1) * HEAD_DIM] = out_t[:, g * bq:(g + 1) * bq].T.astype(BF16)


def _global_attn_kernel(qt_ref, k_ref, vt_ref, o_ref, acc_ref, *, bq, bkc, n_split):
    S = k_ref.shape[2]
    nq = GQA_GROUP * bq
    w = nq // n_split
    acc_ref[...] = jnp.zeros_like(acc_ref)

    def body(c, carry):
        ms, ls = carry
        start = pl.multiple_of(c * bkc, bkc)
        k = k_ref[0, 0, pl.ds(start, bkc), :]
        vt = vt_ref[0, :, pl.ds(start, bkc)]
        new_ms, new_ls = [], []
        for h in range(n_split):
            q = qt_ref[0, 0, :, h * w:(h + 1) * w]
            s = jnp.dot(k, q, preferred_element_type=F32)
            m_new = jnp.maximum(ms[h], jnp.max(s, axis=0, keepdims=True))
            alpha = jnp.exp2(ms[h] - m_new)
            p = jnp.exp2(s - m_new)
            new_ls.append(alpha * ls[h] + jnp.sum(p, axis=0, keepdims=True))
            new_ms.append(m_new)
            pv = jnp.dot(vt, p.astype(BF16), preferred_element_type=F32)
            acc_ref[:, h * w:(h + 1) * w] = alpha * acc_ref[:, h * w:(h + 1) * w] + pv
        return tuple(new_ms), tuple(new_ls)

    init = (tuple(jnp.full((1, w), MASKED, F32) for _ in range(n_split)),
            tuple(jnp.zeros((1, w), F32) for _ in range(n_split)))
    _, ls = lax.fori_loop(0, S // bkc, body, init)
    l = jnp.concatenate(ls, axis=1)
    _store_heads(o_ref, acc_ref[...] / l, bq)


def _global_attn_bounded_kernel(qt_ref, k_ref, vt_ref, o_ref, acc_ref, *, bq, bkc, n_split,
                                unroll):
    S = k_ref.shape[2]
    nq = GQA_GROUP * bq
    w = nq // n_split
    acc_ref[...] = jnp.zeros_like(acc_ref)

    def body(c, ls):
        start = pl.multiple_of(c * bkc, bkc)
        k = k_ref[0, 0, pl.ds(start, bkc), :]
        vt = vt_ref[0, :, pl.ds(start, bkc)]
        ss = [jnp.dot(k, qt_ref[0, 0, :, h * w:(h + 1) * w], preferred_element_type=F32)
              for h in range(n_split)]
        new_ls = []
        for h in range(n_split):
            cols = slice(h * w, (h + 1) * w)
            p = jnp.exp2(ss[h])
            new_ls.append(ls[h] + jnp.sum(p.reshape(bkc // 8, 8, w), axis=0))
            acc_ref[:, cols] += jnp.dot(vt, p.astype(BF16), preferred_element_type=F32)
        return tuple(new_ls)

    ls = lax.fori_loop(0, S // bkc, body, tuple(jnp.zeros((8, w), F32) for _ in range(n_split)),
                       unroll=unroll)
    l = jnp.concatenate([jnp.sum(x, axis=0, keepdims=True) for x in ls], axis=1)
    _store_heads(o_ref, acc_ref[...] / l, bq)


def _global_attention_call(kern, name, qt, k, vt, *, bq):
    B, _, S, _ = k.shape
    nq = GQA_GROUP * bq
    return pl.pallas_call(
        kern,
        grid=(B, N_KV_HEADS, S // bq),
        in_specs=[
            pl.BlockSpec((1, 1, HEAD_DIM, nq), lambda b, h, i: (b, h, 0, i)),
            pl.BlockSpec((1, 1, S, HEAD_DIM), lambda b, h, i: (b, h, 0, 0)),
            pl.BlockSpec((1, HEAD_DIM, S), lambda b, h, i: (b, h, 0)),
        ],
        out_specs=pl.BlockSpec((1, bq, GQA_GROUP * HEAD_DIM), lambda b, h, i: (b, i, h)),
        out_shape=jax.ShapeDtypeStruct((B, S, Q_DIM), BF16),
        scratch_shapes=[pltpu.VMEM((HEAD_DIM, nq), F32)],
        compiler_params=_params(("parallel", "parallel", "arbitrary")),
        name=name,
    )(qt, k, vt)


def _global_attention_bounded(qt, k, vt, *, bq, bkc, n_split, unroll):
    kern = functools.partial(_global_attn_bounded_kernel, bq=bq, bkc=bkc, n_split=n_split,
                             unroll=unroll)
    return _global_attention_call(kern, "global_attention_bounded", qt, k, vt, bq=bq)


def _global_attention(qt, k, vt, *, bq, bkc, n_split):
    kern = functools.partial(_global_attn_kernel, bq=bq, bkc=bkc, n_split=n_split)
    return _global_attention_call(kern, "global_attention", qt, k, vt, bq=bq)


def _window_attn_kernel(qt_ref, k_ref, vt_ref, bias_ref, sink_ref, o_ref, s_ref, *, qb):
    step = pl.program_id(2)
    last_step = pl.num_programs(2) - 1
    nb = k_ref.shape[2] // WINDOW
    nq = GQA_GROUP * WINDOW
    sink = sink_ref[0]

    def block_start(n):
        return pl.multiple_of(n * WINDOW, WINDOW)

    starts = [[block_start(jnp.maximum(step * qb + qi - 1, 0)), block_start(step * qb + qi),
               block_start(jnp.minimum(step * qb + qi + 1, nb - 1))] for qi in range(qb)]
    for qi in range(qb):
        kw = jnp.concatenate([k_ref[0, 0, pl.ds(st, WINDOW), :] for st in starts[qi]], axis=0)
        s_ref[qi] = jnp.dot(kw, qt_ref[0, 0, :, qi * nq:(qi + 1) * nq],
                            preferred_element_type=F32)
    for qi in range(qb):
        vw = jnp.concatenate([vt_ref[0, :, pl.ds(st, WINDOW)] for st in starts[qi]], axis=1)
        s = s_ref[qi] + bias_ref[0]
        if qi == 0:
            s = jnp.concatenate(
                [jnp.where(step == 0, MASKED, s[:WINDOW]), s[WINDOW:]], axis=0)
        if qi == qb - 1:
            s = jnp.concatenate(
                [s[:2 * WINDOW], jnp.where(step == last_step, MASKED, s[2 * WINDOW:])], axis=0)
        m = jnp.maximum(jnp.max(s, axis=0, keepdims=True), sink)
        p = jnp.exp2(s - m)
        l = jnp.sum(p, axis=0, keepdims=True) + jnp.exp2(sink - m)
        out_t = jnp.dot(vw, p.astype(BF16), preferred_element_type=F32) / l
        for g in range(GQA_GROUP):
            o_ref[0, qi * WINDOW:(qi + 1) * WINDOW, g * HEAD_DIM:(g + 1) * HEAD_DIM] = (
                out_t[:, g * WINDOW:(g + 1) * WINDOW].T.astype(BF16))


def _window_attention(qt, k, vt, bias_t, sink_t, *, qb):
    B, _, S, _ = k.shape
    nq = GQA_GROUP * WINDOW
    return pl.pallas_call(
        functools.partial(_window_attn_kernel, qb=qb),
        grid=(B, N_KV_HEADS, S // (qb * WINDOW)),
        in_specs=[
            pl.BlockSpec((1, 1, HEAD_DIM, qb * nq), lambda b, h, n: (b, h, 0, n)),
            pl.BlockSpec((1, 1, S, HEAD_DIM), lambda b, h, n: (b, h, 0, 0)),
            pl.BlockSpec((1, HEAD_DIM, S), lambda b, h, n: (b, h, 0)),
            pl.BlockSpec((1, 3 * WINDOW, nq), lambda b, h, n: (h, 0, 0)),
            pl.BlockSpec((1, 1, nq), lambda b, h, n: (h, 0, 0)),
        ],
        out_specs=pl.BlockSpec((1, qb * WINDOW, GQA_GROUP * HEAD_DIM), lambda b, h, n: (b, n, h)),
        out_shape=jax.ShapeDtypeStruct((B, S, Q_DIM), BF16),
        scratch_shapes=[pltpu.VMEM((qb, 3 * WINDOW, nq), F32)],
        compiler_params=_params(("parallel", "parallel", "parallel")),
        name="window_attention",
    )(qt, k, vt, bias_t, sink_t)


def _layer_norm(y, g, b):
    mu = jnp.mean(y, axis=-1, keepdims=True)
    d = y - mu
    var = jnp.mean(d * d, axis=-1, keepdims=True)
    return d * lax.rsqrt(var + LN_EPS) * g + b


ROW_GROUPS = 2


def _post_attention_kernel(x_ref, o_ref, wo_ref, g1_ref, b1_ref, wg_ref, wu_ref, wd_ref, g2_ref,
                           b2_ref, y_ref):
    tm = x_ref.shape[0]
    rows = [slice(r * tm // ROW_GROUPS, (r + 1) * tm // ROW_GROUPS) for r in range(ROW_GROUPS)]
    hs = [jnp.dot(o_ref[r, :], wo_ref[...], preferred_element_type=F32) for r in rows]
    x1s = [_layer_norm(DEEPNORM_ALPHA * x_ref[r, :] + h, g1_ref[...], b1_ref[...])
           for r, h in zip(rows, hs)]
    mids = []
    for x1 in x1s:
        xb = x1.astype(BF16)
        gate = jnp.dot(xb, wg_ref[...], preferred_element_type=F32)
        up = jnp.dot(xb, wu_ref[...], preferred_element_type=F32)
        mids.append((gate * jax.nn.sigmoid(gate) * up).astype(BF16))
    hs = [jnp.dot(mid, wd_ref[...], preferred_element_type=F32) for mid in mids]
    for r, x1, h in zip(rows, x1s, hs):
        y_ref[r, :] = _layer_norm(DEEPNORM_ALPHA * x1 + h, g2_ref[...], b2_ref[...])


def _post_attention(x, o, wo, g1, b1, wg, wu, wd, g2, b2, *, tm):
    T = x.shape[0]
    vec = _const_spec((1, D_MODEL))
    return pl.pallas_call(
        _post_attention_kernel,
        grid=(T // tm,),
        in_specs=[
            pl.BlockSpec((tm, D_MODEL), lambda i: (i, 0)),
            pl.BlockSpec((tm, Q_DIM), lambda i: (i, 0)),
            _const_spec((Q_DIM, D_MODEL)), vec, vec,
            _const_spec((D_MODEL, D_FF)),
            _const_spec((D_MODEL, D_FF)),
            _const_spec((D_FF, D_MODEL)), vec, vec,
        ],
        out_specs=pl.BlockSpec((tm, D_MODEL), lambda i: (i, 0)),
        out_shape=jax.ShapeDtypeStruct((T, D_MODEL), F32),
        compiler_params=_params(("parallel",)),
        name="post_attention",
    )(x, o, wo, g1, b1, wg, wu, wd, g2, b2)


def _rope_tables_t(seq_len):
    rows_n = seq_len // GRID_W
    rows = jnp.repeat(jnp.arange(rows_n, dtype=F32), GRID_W)
    cols = jnp.tile(jnp.arange(GRID_W, dtype=F32), rows_n)
    inv_freq = ROPE_THETA ** (-jnp.arange(0, AXIS_DIM, 2, dtype=F32) / AXIS_DIM)
    ang_r = (rows[:, None] * inv_freq).T
    ang_c = (cols[:, None] * inv_freq).T
    cos_t = jnp.concatenate([jnp.cos(ang_r)] * 2 + [jnp.cos(ang_c)] * 2, axis=0)
    sin_t = jnp.concatenate([-jnp.sin(ang_r), jnp.sin(ang_r), -jnp.sin(ang_c), jnp.sin(ang_c)],
                            axis=0)
    return cos_t, sin_t


def _rope_partner(g):
    h = AXIS_DIM // 2
    return jnp.concatenate([g[h:2 * h], g[0:h], g[3 * h:4 * h], g[2 * h:3 * h]])


def _t5_bucket(rel):
    nb = N_BUCKETS // 2
    max_exact = nb // 2
    base = (rel > 0).astype(jnp.int32) * nb
    n = jnp.abs(rel)
    nf = jnp.maximum(n, max_exact).astype(F32)
    large = max_exact + (jnp.log(nf / max_exact) / math.log(MAX_DISTANCE / max_exact)
                         * (nb - max_exact)).astype(jnp.int32)
    large = jnp.minimum(large, nb - 1)
    return base + jnp.where(n < max_exact, n, large)


def _window_bias_t(rel_bias_table):
    r = jnp.arange(WINDOW)[None, :]
    c = jnp.arange(3 * WINDOW)[:, None]
    rel = (c - WINDOW) - r
    bucket = _t5_bucket(rel)[..., None]
    table = rel_bias_table.astype(F32)
    bias = sum(jnp.where(bucket == b, table[b], 0.0) for b in range(N_BUCKETS)) * LOG2E
    bias = jnp.where((jnp.abs(rel) <= WINDOW)[..., None], bias, MASKED)
    bias = bias.transpose(2, 0, 1).reshape(N_KV_HEADS, GQA_GROUP, 3 * WINDOW, WINDOW)
    return bias.transpose(0, 2, 1, 3).reshape(N_KV_HEADS, 3 * WINDOW, GQA_GROUP * WINDOW)


def _trunk(x, a_w_qkv, a_q_gain, a_k_gain, a_w_o, b_w_qkv, b_sink, b_w_o, rel_bias_table,
           ln1_g, ln1_b, w_gate, w_up, w_down, ln2_g, ln2_b):
    B, S, _ = x.shape
    T = B * S
    tm_qkv = 512
    tm_tok = 512
    cos_t, sin_t = _rope_tables_t(S)
    for i in range(DEPTH):
        j = i // 2
        if i % 2 == 0:
            bq = 512
            gq = a_q_gain[j].astype(F32) * Q_PRESCALE
            gk = a_k_gain[j].astype(F32)
            tables = (gq[:, None] * cos_t, _rope_partner(gq)[:, None] * sin_t,
                      gk[:, None] * cos_t, _rope_partner(gk)[:, None] * sin_t)
            qt, k, vt = _qkv_project(x, a_w_qkv[j].T.astype(BF16), tables, bq=bq, tm=tm_qkv)
            bound = HEAD_DIM * jnp.max(jnp.abs(gq)) * jnp.max(jnp.abs(gk)) * BF16_ROUNDING_SLACK
            attn = functools.partial(_global_attention, bq=bq, bkc=512, n_split=4)
            attn_bounded = functools.partial(_global_attention_bounded, bq=bq, bkc=1024, n_split=4,
                                             unroll=4)
            o = lax.cond(bound <= SAFE_LOG2_SPAN, attn_bounded, attn, qt, k, vt)
            wo = a_w_o[j]
        else:
            qt, k, vt = _qkv_project(x, b_w_qkv[j].T.astype(BF16), (), bq=WINDOW, tm=tm_qkv)
            bias_t = _window_bias_t(rel_bias_table)
            sink_t = jnp.repeat(b_sink[j].astype(F32) * LOG2E, WINDOW).reshape(
                N_KV_HEADS, 1, GQA_GROUP * WINDOW)
            o = _window_attention(qt, k, vt, bias_t, sink_t, qb=4)
            wo = b_w_o[j]
        x2 = _post_attention(x.reshape(T, D_MODEL), o.reshape(T, Q_DIM), wo.astype(BF16),
                             ln1_g[i][None], ln1_b[i][None], w_gate[i].astype(BF16),
                             w_up[i].astype(BF16), w_down[i].astype(BF16),
                             ln2_g[i][None], ln2_b[i][None], tm=tm_tok)
        x = x2.reshape(B, S, D_MODEL)
    return x


def kernel(x_prompt, x_sample, a_w_qkv, a_q_gain, a_k_gain, a_w_o, b_w_qkv, b_sink, b_w_o,
           rel_bias_table, ln1_g, ln1_b, w_gate, w_up, w_down, ln2_g, ln2_b):
    weights = (a_w_qkv, a_q_gain, a_k_gain, a_w_o, b_w_qkv, b_sink, b_w_o, rel_bias_table,
               ln1_g, ln1_b, w_gate, w_up, w_down, ln2_g, ln2_b)
    return (_trunk(x_prompt, *weights), _trunk(x_sample, *weights))
```

```python
import functools
import math

import jax
import jax.numpy as jnp
from jax import lax
from jax.experimental import pallas as pl
from jax.experimental.pallas import tpu as pltpu

D_MODEL = 1024
DEPTH = 2
N_HEADS = 8
N_KV_HEADS = 2
HEAD_DIM = 128
GQA_GROUP = N_HEADS // N_KV_HEADS
Q_DIM = N_HEADS * HEAD_DIM
KV_DIM = N_KV_HEADS * HEAD_DIM
QKV_DIM = Q_DIM + 2 * KV_DIM
D_FF = 2816
GRID_W = 64
AXIS_DIM = HEAD_DIM // 2
ROPE_THETA = 10000.0
WINDOW = 128
N_BUCKETS = 32
MAX_DISTANCE = 128
LN_EPS = 1e-5
RMS_EPS = 1e-6
DEEPNORM_ALPHA = (2.0 * DEPTH) ** 0.25
LOG2E = math.log2(math.e)
Q_PRESCALE = HEAD_DIM ** -0.5 * LOG2E
MASKED = -1e30
SAFE_LOG2_SPAN = 48.0
BF16_ROUNDING_SLACK = 1.02

V7X_VMEM_LIMIT_BYTES = 56 * 1024 * 1024

BF16 = jnp.bfloat16
F32 = jnp.float32


def _params(sem):
    return pltpu.CompilerParams(dimension_semantics=sem, vmem_limit_bytes=V7X_VMEM_LIMIT_BYTES)


def _const_spec(shape):
    nd = len(shape)
    return pl.BlockSpec(shape, lambda *_: (0,) * nd, pipeline_mode=pl.Buffered(1))


HEADS_PER_DOT = 2


def _qkv_kernel(x_ref, wt_ref, *refs, norm_rope, bq):
    if norm_rope:
        cq_ref, sq_ref, ck_ref, sk_ref, qt_ref, k_ref, vt_ref = refs
    else:
        qt_ref, k_ref, vt_ref = refs
    tm = x_ref.shape[1]
    xb = x_ref[0].astype(BF16)

    def norm_rope_slab(y, cos_ref, sin_ref):
        r = lax.rsqrt(jnp.mean(y * y, axis=0, keepdims=True) + RMS_EPS)
        h = AXIS_DIM // 2
        partner = jnp.concatenate([y[h:2 * h], y[0:h], y[3 * h:4 * h], y[2 * h:3 * h]], axis=0)
        return (y * cos_ref[...] + partner * sin_ref[...]) * r

    rows = HEADS_PER_DOT * HEAD_DIM
    for grp in range(QKV_DIM // rows):
        yt = lax.dot_general(wt_ref[grp * rows:(grp + 1) * rows, :], xb, (((1,), (1,)), ((), ())),
                             preferred_element_type=F32)
        for sub in range(HEADS_PER_DOT):
            slab = yt[sub * HEAD_DIM:(sub + 1) * HEAD_DIM]
            head = grp * HEADS_PER_DOT + sub
            if head < N_HEADS:
                slab = norm_rope_slab(slab, cq_ref, sq_ref) if norm_rope else slab * Q_PRESCALE
                slab = slab.astype(BF16)
                kvh, g = divmod(head, GQA_GROUP)
                for j in range(tm // bq):
                    col = (j * GQA_GROUP + g) * bq
                    qt_ref[0, kvh, :, col:col + bq] = slab[:, j * bq:(j + 1) * bq]
            elif head < N_HEADS + N_KV_HEADS:
                kvh = head - N_HEADS
                if norm_rope:
                    slab = norm_rope_slab(slab, ck_ref, sk_ref)
                k_ref[0, kvh] = slab.T.astype(BF16)
            else:
                kvh = head - N_HEADS - N_KV_HEADS
                vt_ref[0, kvh * HEAD_DIM:(kvh + 1) * HEAD_DIM, :] = slab.astype(BF16)


def _qkv_project(x, wt, tables, *, bq, tm):
    B, S, _ = x.shape
    kern = functools.partial(_qkv_kernel, norm_rope=bool(tables), bq=bq)
    return pl.pallas_call(
        kern,
        grid=(B, S // tm),
        in_specs=[
            pl.BlockSpec((1, tm, D_MODEL), lambda b, i: (b, i, 0)),
            _const_spec((QKV_DIM, D_MODEL)),
        ] + [pl.BlockSpec((HEAD_DIM, tm), lambda b, i: (0, i)) for _ in tables],
        out_specs=[
            pl.BlockSpec((1, N_KV_HEADS, HEAD_DIM, GQA_GROUP * tm), lambda b, i: (b, 0, 0, i)),
            pl.BlockSpec((1, N_KV_HEADS, tm, HEAD_DIM), lambda b, i: (b, 0, i, 0)),
            pl.BlockSpec((1, KV_DIM, tm), lambda b, i: (b, 0, i)),
        ],
        out_shape=[
            jax.ShapeDtypeStruct((B, N_KV_HEADS, HEAD_DIM, GQA_GROUP * S), BF16),
            jax.ShapeDtypeStruct((B, N_KV_HEADS, S, HEAD_DIM), BF16),
            jax.ShapeDtypeStruct((B, KV_DIM, S), BF16),
        ],
        compiler_params=_params(("parallel", "parallel")),
        name="qkv_project",
    )(x, wt, *tables)


def _store_heads(o_ref, out_t, bq):
    for g in range(GQA_GROUP):
        o_ref[0, :, g * HEAD_DIM:(g + 1) * HEAD_DIM] = out_t[:, g * bq:(g + 1) * bq].T.astype(BF16)


def _global_attn_kernel(qt_ref, k_ref, vt_ref, o_ref, acc_ref, *, bq, bkc, n_split):
    S = k_ref.shape[2]
    nq = GQA_GROUP * bq
    w = nq // n_split
    acc_ref[...] = jnp.zeros_like(acc_ref)

    def body(c, carry):
        ms, ls = carry
        start = pl.multiple_of(c * bkc, bkc)
        k = k_ref[0, 0, pl.ds(start, bkc), :]
        vt = vt_ref[0, :, pl.ds(start, bkc)]
        new_ms, new_ls = [], []
        for h in range(n_split):
            q = qt_ref[0, 0, :, h * w:(h + 1) * w]
            s = jnp.dot(k, q, preferred_element_type=F32)
            m_new = jnp.maximum(ms[h], jnp.max(s, axis=0, keepdims=True))
            alpha = jnp.exp2(ms[h] - m_new)
            p = jnp.exp2(s - m_new)
            new_ls.append(alpha * ls[h] + jnp.sum(p, axis=0, keepdims=True))
            new_ms.append(m_new)
            pv = jnp.dot(vt, p.astype(BF16), preferred_element_type=F32)
            acc_ref[:, h * w:(h + 1) * w] = alpha * acc_ref[:, h * w:(h + 1) * w] + pv
        return tuple(new_ms), tuple(new_ls)

    init = (tuple(jnp.full((1, w), MASKED, F32) for _ in range(n_split)),
            tuple(jnp.zeros((1, w), F32) for _ in range(n_split)))
    _, ls = lax.fori_loop(0, S // bkc, body, init)
    l = jnp.concatenate(ls, axis=1)
    _store_heads(o_ref, acc_ref[...] / l, bq)


def _global_attn_bounded_kernel(qt_ref, k_ref, vt_ref, o_ref, acc_ref, *, bq, bkc, n_split,
                                unroll):
    S = k_ref.shape[2]
    nq = GQA_GROUP * bq
    w = nq // n_split
    acc_ref[...] = jnp.zeros_like(acc_ref)

    def body(c, ls):
        start = pl.multiple_of(c * bkc, bkc)
        k = k_ref[0, 0, pl.ds(start, bkc), :]
        vt = vt_ref[0, :, pl.ds(start, bkc)]
        ss = [jnp.dot(k, qt_ref[0, 0, :, h * w:(h + 1) * w], preferred_element_type=F32)
              for h in range(n_split)]
        new_ls = []
        for h in range(n_split):
            cols = slice(h * w, (h + 1) * w)
            p = jnp.exp2(ss[h])
            new_ls.append(ls[h] + jnp.sum(p.reshape(bkc // 8, 8, w), axis=0))
            acc_ref[:, cols] += jnp.dot(vt, p.astype(BF16), preferred_element_type=F32)
        return tuple(new_ls)

    ls = lax.fori_loop(0, S // bkc, body, tuple(jnp.zeros((8, w), F32) for _ in range(n_split)),
                       unroll=unroll)
    l = jnp.concatenate([jnp.sum(x, axis=0, keepdims=True) for x in ls], axis=1)
    _store_heads(o_ref, acc_ref[...] / l, bq)


def _global_attention_call(kern, name, qt, k, vt, *, bq):
    B, _, S, _ = k.shape
    nq = GQA_GROUP * bq
    return pl.pallas_call(
        kern,
        grid=(B, N_KV_HEADS, S // bq),
        in_specs=[
            pl.BlockSpec((1, 1, HEAD_DIM, nq), lambda b, h, i: (b, h, 0, i)),
            pl.BlockSpec((1, 1, S, HEAD_DIM), lambda b, h, i: (b, h, 0, 0)),
            pl.BlockSpec((1, HEAD_DIM, S), lambda b, h, i: (b, h, 0)),
        ],
        out_specs=pl.BlockSpec((1, bq, GQA_GROUP * HEAD_DIM), lambda b, h, i: (b, i, h)),
        out_shape=jax.ShapeDtypeStruct((B, S, Q_DIM), BF16),
        scratch_shapes=[pltpu.VMEM((HEAD_DIM, nq), F32)],
        compiler_params=_params(("parallel", "parallel", "arbitrary")),
        name=name,
    )(qt, k, vt)


def _global_attention_bounded(qt, k, vt, *, bq, bkc, n_split, unroll):
    kern = functools.partial(_global_attn_bounded_kernel, bq=bq, bkc=bkc, n_split=n_split,
                             unroll=unroll)
    return _global_attention_call(kern, "global_attention_bounded", qt, k, vt, bq=bq)


def _global_attention(qt, k, vt, *, bq, bkc, n_split):
    kern = functools.partial(_global_attn_kernel, bq=bq, bkc=bkc, n_split=n_split)
    return _global_attention_call(kern, "global_attention", qt, k, vt, bq=bq)


def _window_attn_kernel(qt_ref, k_ref, vt_ref, bias_ref, sink_ref, o_ref, s_ref, *, qb):
    step = pl.program_id(2)
    last_step = pl.num_programs(2) - 1
    nb = k_ref.shape[2] // WINDOW
    nq = GQA_GROUP * WINDOW
    sink = sink_ref[0]

    def block_start(n):
        return pl.multiple_of(n * WINDOW, WINDOW)

    starts = [[block_start(jnp.maximum(step * qb + qi - 1, 0)), block_start(step * qb + qi),
               block_start(jnp.minimum(step * qb + qi + 1, nb - 1))] for qi in range(qb)]
    for qi in range(qb):
        kw = jnp.concatenate([k_ref[0, 0, pl.ds(st, WINDOW), :] for st in starts[qi]], axis=0)
        s_ref[qi] = jnp.dot(kw, qt_ref[0, 0, :, qi * nq:(qi + 1) * nq],
                            preferred_element_type=F32)
    for qi in range(qb):
        vw = jnp.concatenate([vt_ref[0, :, pl.ds(st, WINDOW)] for st in starts[qi]], axis=1)
        s = s_ref[qi] + bias_ref[0]
        if qi == 0:
            s = jnp.concatenate(
                [jnp.where(step == 0, MASKED, s[:WINDOW]), s[WINDOW:]], axis=0)
        if qi == qb - 1:
            s = jnp.concatenate(
                [s[:2 * WINDOW], jnp.where(step == last_step, MASKED, s[2 * WINDOW:])], axis=0)
        m = jnp.maximum(jnp.max(s, axis=0, keepdims=True), sink)
        p = jnp.exp2(s - m)
        l = jnp.sum(p, axis=0, keepdims=True) + jnp.exp2(sink - m)
        out_t = jnp.dot(vw, p.astype(BF16), preferred_element_type=F32) / l
        for g in range(GQA_GROUP):
            o_ref[0, qi * WINDOW:(qi + 1) * WINDOW, g * HEAD_DIM:(g + 1) * HEAD_DIM] = (
                out_t[:, g * WINDOW:(g + 1) * WINDOW].T.astype(BF16))


def _window_attention(qt, k, vt, bias_t, sink_t, *, qb):
    B, _, S, _ = k.shape
    nq = GQA_GROUP * WINDOW
    return pl.pallas_call(
        functools.partial(_window_attn_kernel, qb=qb),
        grid=(B, N_KV_HEADS, S // (qb * WINDOW)),
        in_specs=[
            pl.BlockSpec((1, 1, HEAD_DIM, qb * nq), lambda b, h, n: (b, h, 0, n)),
            pl.BlockSpec((1, 1, S, HEAD_DIM), lambda b, h, n: (b, h, 0, 0)),
            pl.BlockSpec((1, HEAD_DIM, S), lambda b, h, n: (b, h, 0)),
            pl.BlockSpec((1, 3 * WINDOW, nq), lambda b, h, n: (h, 0, 0)),
            pl.BlockSpec((1, 1, nq), lambda b, h, n: (h, 0, 0)),
        ],
        out_specs=pl.BlockSpec((1, qb * WINDOW, GQA_GROUP * HEAD_DIM), lambda b, h, n: (b, n, h)),
        out_shape=jax.ShapeDtypeStruct((B, S, Q_DIM), BF16),
        scratch_shapes=[pltpu.VMEM((qb, 3 * WINDOW, nq), F32)],
        compiler_params=_params(("parallel", "parallel", "parallel")),
        name="window_attention",
    )(qt, k, vt, bias_t, sink_t)


def _layer_norm(y, g, b):
    mu = jnp.mean(y, axis=-1, keepdims=True)
    d = y - mu
    var = jnp.mean(d * d, axis=-1, keepdims=True)
    return d * lax.rsqrt(var + LN_EPS) * g + b


ROW_GROUPS = 4


def _post_attention_kernel(x_ref, o_ref, wo_ref, g1_ref, b1_ref, wg_ref, wu_ref, wd_ref, g2_ref,
                           b2_ref, y_ref):
    tm = x_ref.shape[0]
    rows = [slice(r * tm // ROW_GROUPS, (r + 1) * tm // ROW_GROUPS) for r in range(ROW_GROUPS)]
    hs = [jnp.dot(o_ref[r, :], wo_ref[...], preferred_element_type=F32) for r in rows]
    x1s = [_layer_norm(DEEPNORM_ALPHA * x_ref[r, :] + h, g1_ref[...], b1_ref[...])
           for r, h in zip(rows, hs)]
    mids = []
    for x1 in x1s:
        xb = x1.astype(BF16)
        gate = jnp.dot(xb, wg_ref[...], preferred_element_type=F32)
        up = jnp.dot(xb, wu_ref[...], preferred_element_type=F32)
        mids.append((gate * jax.nn.sigmoid(gate) * up).astype(BF16))
    hs = [jnp.dot(mid, wd_ref[...], preferred_element_type=F32) for mid in mids]
    for r, x1, h in zip(rows, x1s, hs):
        y_ref[r, :] = _layer_norm(DEEPNORM_ALPHA * x1 + h, g2_ref[...], b2_ref[...])


def _post_attention(x, o, wo, g1, b1, wg, wu, wd, g2, b2, *, tm):
    T = x.shape[0]
    vec = _const_spec((1, D_MODEL))
    return pl.pallas_call(
        _post_attention_kernel,
        grid=(T // tm,),
        in_specs=[
            pl.BlockSpec((tm, D_MODEL), lambda i: (i, 0)),
            pl.BlockSpec((tm, Q_DIM), lambda i: (i, 0)),
            _const_spec((Q_DIM, D_MODEL)), vec, vec,
            _const_spec((D_MODEL, D_FF)),
            _const_spec((D_MODEL, D_FF)),
            _const_spec((D_FF, D_MODEL)), vec, vec,
        ],
        out_specs=pl.BlockSpec((tm, D_MODEL), lambda i: (i, 0)),
        out_shape=jax.ShapeDtypeStruct((T, D_MODEL), F32),
        compiler_params=_params(("parallel",)),
        name="post_attention",
    )(x, o, wo, g1, b1, wg, wu, wd, g2, b2)


def _rope_tables_t(seq_len):
    rows_n = seq_len // GRID_W
    rows = jnp.repeat(jnp.arange(rows_n, dtype=F32), GRID_W)
    cols = jnp.tile(jnp.arange(GRID_W, dtype=F32), rows_n)
    inv_freq = ROPE_THETA ** (-jnp.arange(0, AXIS_DIM, 2, dtype=F32) / AXIS_DIM)
    ang_r = (rows[:, None] * inv_freq).T
    ang_c = (cols[:, None] * inv_freq).T
    cos_t = jnp.concatenate([jnp.cos(ang_r)] * 2 + [jnp.cos(ang_c)] * 2, axis=0)
    sin_t = jnp.concatenate([-jnp.sin(ang_r), jnp.sin(ang_r), -jnp.sin(ang_c), jnp.sin(ang_c)],
                            axis=0)
    return cos_t, sin_t


def _rope_partner(g):
    h = AXIS_DIM // 2
    return jnp.concatenate([g[h:2 * h], g[0:h], g[3 * h:4 * h], g[2 * h:3 * h]])


def _t5_bucket(rel):
    nb = N_BUCKETS // 2
    max_exact = nb // 2
    base = (rel > 0).astype(jnp.int32) * nb
    n = jnp.abs(rel)
    nf = jnp.maximum(n, max_exact).astype(F32)
    large = max_exact + (jnp.log(nf / max_exact) / math.log(MAX_DISTANCE / max_exact)
                         * (nb - max_exact)).astype(jnp.int32)
    large = jnp.minimum(large, nb - 1)
    return base + jnp.where(n < max_exact, n, large)


def _window_bias_t(rel_bias_table):
    r = jnp.arange(WINDOW)[None, :]
    c = jnp.arange(3 * WINDOW)[:, None]
    rel = (c - WINDOW) - r
    bucket = _t5_bucket(rel)[..., None]
    table = rel_bias_table.astype(F32)
    bias = sum(jnp.where(bucket == b, table[b], 0.0) for b in range(N_BUCKETS)) * LOG2E
    bias = jnp.where((jnp.abs(rel) <= WINDOW)[..., None], bias, MASKED)
    bias = bias.transpose(2, 0, 1).reshape(N_KV_HEADS, GQA_GROUP, 3 * WINDOW, WINDOW)
    return bias.transpose(0, 2, 1, 3).reshape(N_KV_HEADS, 3 * WINDOW, GQA_GROUP * WINDOW)


def _trunk(x, a_w_qkv, a_q_gain, a_k_gain, a_w_o, b_w_qkv, b_sink, b_w_o, rel_bias_table,
           ln1_g, ln1_b, w_gate, w_up, w_down, ln2_g, ln2_b):
    B, S, _ = x.shape
    T = B * S
    tm_qkv = 1024
    tm_tok = 1024
    cos_t, sin_t = _rope_tables_t(S)
    for i in range(DEPTH):
        j = i // 2
        if i % 2 == 0:
            bq = 512
            gq = a_q_gain[j].astype(F32) * Q_PRESCALE
            gk = a_k_gain[j].astype(F32)
            tables = (gq[:, None] * cos_t, _rope_partner(gq)[:, None] * sin_t,
                      gk[:, None] * cos_t, _rope_partner(gk)[:, None] * sin_t)
            qt, k, vt = _qkv_project(x, a_w_qkv[j].T.astype(BF16), tables, bq=bq, tm=tm_qkv)
            bound = HEAD_DIM * jnp.max(jnp.abs(gq)) * jnp.max(jnp.abs(gk)) * BF16_ROUNDING_SLACK
            attn = functools.partial(_global_attention, bq=bq, bkc=512, n_split=4)
            attn_bounded = functools.partial(_global_attention_bounded, bq=bq, bkc=1024, n_split=4,
                                             unroll=4)
            o = lax.cond(bound <= SAFE_LOG2_SPAN, attn_bounded, attn, qt, k, vt)
            wo = a_w_o[j]
        else:
            qt, k, vt = _qkv_project(x, b_w_qkv[j].T.astype(BF16), (), bq=WINDOW, tm=tm_qkv)
            bias_t = _window_bias_t(rel_bias_table)
            sink_t = jnp.repeat(b_sink[j].astype(F32) * LOG2E, WINDOW).reshape(
                N_KV_HEADS, 1, GQA_GROUP * WINDOW)
            o = _window_attention(qt, k, vt, bias_t, sink_t, qb=8)
            wo = b_w_o[j]
        x2 = _post_attention(x.reshape(T, D_MODEL), o.reshape(T, Q_DIM), wo.astype(BF16),
                             ln1_g[i][None], ln1_b[i][None], w_gate[i].astype(BF16),
                             w_up[i].astype(BF16), w_down[i].astype(BF16),
                             ln2_g[i][None], ln2_b[i][None], tm=tm_tok)
        x = x2.reshape(B, S, D_MODEL)
    return x


def kernel(x_prompt, x_sample, a_w_qkv, a_q_gain, a_k_gain, a_w_o, b_w_qkv, b_sink, b_w_o,
           rel_bias_table, ln1_g, ln1_b, w_gate, w_up, w_down, ln2_g, ln2_b):
    weights = (a_w_qkv, a_q_gain, a_k_gain, a_w_o, b_w_qkv, b_sink, b_w_o, rel_bias_table,
               ln1_g, ln1_b, w_gate, w_up, w_down, ln2_g, ln2_b)
    return (_trunk(x_prompt, *weights), _trunk(x_sample, *weights))
```

```python
import functools
import math

import jax
import jax.numpy as jnp
from jax import lax
from jax.experimental import pallas as pl
from jax.experimental.pallas import tpu as pltpu

D_MODEL = 1024
DEPTH = 2
N_HEADS = 8
N_KV_HEADS = 2
HEAD_DIM = 128
GQA_GROUP = N_HEADS // N_KV_HEADS
Q_DIM = N_HEADS * HEAD_DIM
KV_DIM = N_KV_HEADS * HEAD_DIM
QKV_DIM = Q_DIM + 2 * KV_DIM
D_FF = 2816
GRID_W = 64
AXIS_DIM = HEAD_DIM // 2
ROPE_THETA = 10000.0
WINDOW = 128
N_BUCKETS = 32
MAX_DISTANCE = 128
LN_EPS = 1e-5
RMS_EPS = 1e-6
DEEPNORM_ALPHA = (2.0 * DEPTH) ** 0.25
LOG2E = math.log2(math.e)
Q_PRESCALE = HEAD_DIM ** -0.5 * LOG2E
MASKED = -1e30
SAFE_LOG2_SPAN = 48.0
BF16_ROUNDING_SLACK = 1.02

V7X_VMEM_LIMIT_BYTES = 56 * 1024 * 1024

BF16 = jnp.bfloat16
F32 = jnp.float32


def _params(sem):
    return pltpu.CompilerParams(dimension_semantics=sem, vmem_limit_bytes=V7X_VMEM_LIMIT_BYTES)


def _const_spec(shape):
    nd = len(shape)
    return pl.BlockSpec(shape, lambda *_: (0,) * nd, pipeline_mode=pl.Buffered(1))


HEADS_PER_DOT = 2


def _qkv_kernel(x_ref, wt_ref, *refs, norm_rope, bq):
    if norm_rope:
        cq_ref, sq_ref, ck_ref, sk_ref, qt_ref, k_ref, vt_ref = refs
    else:
        qt_ref, k_ref, vt_ref = refs
    tm = x_ref.shape[1]
    xb = x_ref[0].astype(BF16)

    def norm_rope_slab(y, cos_ref, sin_ref):
        r = lax.rsqrt(jnp.mean(y * y, axis=0, keepdims=True) + RMS_EPS)
        h = AXIS_DIM // 2
        partner = jnp.concatenate([y[h:2 * h], y[0:h], y[3 * h:4 * h], y[2 * h:3 * h]], axis=0)
        return (y * cos_ref[...] + partner * sin_ref[...]) * r

    rows = HEADS_PER_DOT * HEAD_DIM
    for grp in range(QKV_DIM // rows):
        yt = lax.dot_general(wt_ref[grp * rows:(grp + 1) * rows, :], xb, (((1,), (1,)), ((), ())),
                             preferred_element_type=F32)
        for sub in range(HEADS_PER_DOT):
            slab = yt[sub * HEAD_DIM:(sub + 1) * HEAD_DIM]
            head = grp * HEADS_PER_DOT + sub
            if head < N_HEADS:
                slab = norm_rope_slab(slab, cq_ref, sq_ref) if norm_rope else slab * Q_PRESCALE
                slab = slab.astype(BF16)
                kvh, g = divmod(head, GQA_GROUP)
                for j in range(tm // bq):
                    col = (j * GQA_GROUP + g) * bq
                    qt_ref[0, kvh, :, col:col + bq] = slab[:, j * bq:(j + 1) * bq]
            elif head < N_HEADS + N_KV_HEADS:
                kvh = head - N_HEADS
                if norm_rope:
                    slab = norm_rope_slab(slab, ck_ref, sk_ref)
                k_ref[0, kvh] = slab.T.astype(BF16)
            else:
                kvh = head - N_HEADS - N_KV_HEADS
                vt_ref[0, kvh * HEAD_DIM:(kvh + 1) * HEAD_DIM, :] = slab.astype(BF16)


def _qkv_project(x, wt, tables, *, bq, tm):
    B, S, _ = x.shape
    kern = functools.partial(_qkv_kernel, norm_rope=bool(tables), bq=bq)
    return pl.pallas_call(
        kern,
        grid=(B, S // tm),
        in_specs=[
            pl.BlockSpec((1, tm, D_MODEL), lambda b, i: (b, i, 0)),
            _const_spec((QKV_DIM, D_MODEL)),
        ] + [pl.BlockSpec((HEAD_DIM, tm), lambda b, i: (0, i)) for _ in tables],
        out_specs=[
            pl.BlockSpec((1, N_KV_HEADS, HEAD_DIM, GQA_GROUP * tm), lambda b, i: (b, 0, 0, i)),
            pl.BlockSpec((1, N_KV_HEADS, tm, HEAD_DIM), lambda b, i: (b, 0, i, 0)),
            pl.BlockSpec((1, KV_DIM, tm), lambda b, i: (b, 0, i)),
        ],
        out_shape=[
            jax.ShapeDtypeStruct((B, N_KV_HEADS, HEAD_DIM, GQA_GROUP * S), BF16),
            jax.ShapeDtypeStruct((B, N_KV_HEADS, S, HEAD_DIM), BF16),
            jax.ShapeDtypeStruct((B, KV_DIM, S), BF16),
        ],
        compiler_params=_params(("parallel", "parallel")),
        name="qkv_project",
    )(x, wt, *tables)


def _store_heads(o_ref, out_t, bq):
    for g in range(GQA_GROUP):
        o_ref[0, :, g * HEAD_DIM:(g + 1) * HEAD_DIM] = out_t[:, g * bq:(g + 1) * bq].T.astype(BF16)


def _global_attn_kernel(qt_ref, k_ref, vt_ref, o_ref, acc_ref, *, bq, bkc, n_split):
    S = k_ref.shape[2]
    nq = GQA_GROUP * bq
    w = nq // n_split
    acc_ref[...] = jnp.zeros_like(acc_ref)

    def body(c, carry):
        ms, ls = carry
        start = pl.multiple_of(c * bkc, bkc)
        k = k_ref[0, 0, pl.ds(start, bkc), :]
        vt = vt_ref[0, :, pl.ds(start, bkc)]
        new_ms, new_ls = [], []
        for h in range(n_split):
            q = qt_ref[0, 0, :, h * w:(h + 1) * w]
            s = jnp.dot(k, q, preferred_element_type=F32)
            m_new = jnp.maximum(ms[h], jnp.max(s, axis=0, keepdims=True))
            alpha = jnp.exp2(ms[h] - m_new)
            p = jnp.exp2(s - m_new)
            new_ls.append(alpha * ls[h] + jnp.sum(p, axis=0, keepdims=True))
            new_ms.append(m_new)
            pv = jnp.dot(vt, p.astype(BF16), preferred_element_type=F32)
            acc_ref[:, h * w:(h + 1) * w] = alpha * acc_ref[:, h * w:(h + 1) * w] + pv
        return tuple(new_ms), tuple(new_ls)

    init = (tuple(jnp.full((1, w), MASKED, F32) for _ in range(n_split)),
            tuple(jnp.zeros((1, w), F32) for _ in range(n_split)))
    _, ls = lax.fori_loop(0, S // bkc, body, init)
    l = jnp.concatenate(ls, axis=1)
    _store_heads(o_ref, acc_ref[...] * (1.0 / l), bq)


def _global_attn_bounded_kernel(qt_ref, k_ref, vt_ref, o_ref, acc_ref, *, bq, bkc, n_split,
                                unroll):
    S = k_ref.shape[2]
    nq = GQA_GROUP * bq
    w = nq // n_split
    acc_ref[...] = jnp.zeros_like(acc_ref)

    def body(c, ls):
        start = pl.multiple_of(c * bkc, bkc)
        k = k_ref[0, 0, pl.ds(start, bkc), :]
        vt = vt_ref[0, :, pl.ds(start, bkc)]
        ss = [jnp.dot(k, qt_ref[0, 0, :, h * w:(h + 1) * w], preferred_element_type=F32)
              for h in range(n_split)]
        new_ls = []
        for h in range(n_split):
            cols = slice(h * w, (h + 1) * w)
            p = jnp.exp2(ss[h])
            new_ls.append(ls[h] + jnp.sum(p.reshape(bkc // 8, 8, w), axis=0))
            acc_ref[:, cols] += jnp.dot(vt, p.astype(BF16), preferred_element_type=F32)
        return tuple(new_ls)

    ls = lax.fori_loop(0, S // bkc, body, tuple(jnp.zeros((8, w), F32) for _ in range(n_split)),
                       unroll=unroll)
    l = jnp.concatenate([jnp.sum(x, axis=0, keepdims=True) for x in ls], axis=1)
    _store_heads(o_ref, acc_ref[...] * (1.0 / l), bq)


def _global_attention_call(kern, name, qt, k, vt, *, bq):
    B, _, S, _ = k.shape
    nq = GQA_GROUP * bq
    return pl.pallas_call(
        kern,
        grid=(B, N_KV_HEADS, S // bq),
        in_specs=[
            pl.BlockSpec((1, 1, HEAD_DIM, nq), lambda b, h, i: (b, h, 0, i)),
            pl.BlockSpec((1, 1, S, HEAD_DIM), lambda b, h, i: (b, h, 0, 0)),
            pl.BlockSpec((1, HEAD_DIM, S), lambda b, h, i: (b, h, 0)),
        ],
        out_specs=pl.BlockSpec((1, bq, GQA_GROUP * HEAD_DIM), lambda b, h, i: (b, i, h)),
        out_shape=jax.ShapeDtypeStruct((B, S, Q_DIM), BF16),
        scratch_shapes=[pltpu.VMEM((HEAD_DIM, nq), F32)],
        compiler_params=_params(("parallel", "parallel", "arbitrary")),
        name=name,
    )(qt, k, vt)


def _global_attention_bounded(qt, k, vt, *, bq, bkc, n_split, unroll):
    kern = functools.partial(_global_attn_bounded_kernel, bq=bq, bkc=bkc, n_split=n_split,
                             unroll=unroll)
    return _global_attention_call(kern, "global_attention_bounded", qt, k, vt, bq=bq)


def _global_attention(qt, k, vt, *, bq, bkc, n_split):
    kern = functools.partial(_global_attn_kernel, bq=bq, bkc=bkc, n_split=n_split)
    return _global_attention_call(kern, "global_attention", qt, k, vt, bq=bq)


def _window_attn_kernel(qt_ref, k_ref, vt_ref, bias_ref, sink_ref, o_ref, s_ref, *, qb):
    step = pl.program_id(2)
    last_step = pl.num_programs(2) - 1
    nb = k_ref.shape[2] // WINDOW
    nq = GQA_GROUP * WINDOW
    sink = sink_ref[0]

    def block_start(n):
        return pl.multiple_of(n * WINDOW, WINDOW)

    starts = [[block_start(jnp.maximum(step * qb + qi - 1, 0)), block_start(step * qb + qi),
               block_start(jnp.minimum(step * qb + qi + 1, nb - 1))] for qi in range(qb)]
    for qi in range(qb):
        kw = jnp.concatenate([k_ref[0, 0, pl.ds(st, WINDOW), :] for st in starts[qi]], axis=0)
        s = jnp.dot(kw, qt_ref[0, 0, :, qi * nq:(qi + 1) * nq],
                    preferred_element_type=F32) + bias_ref[0]
        if qi == 0:
            s = jnp.concatenate(
                [jnp.where(step == 0, MASKED, s[:WINDOW]), s[WINDOW:]], axis=0)
        if qi == qb - 1:
            s = jnp.concatenate(
                [s[:2 * WINDOW], jnp.where(step == last_step, MASKED, s[2 * WINDOW:])], axis=0)
        s_ref[qi] = s
    for qi in range(qb):
        vw = jnp.concatenate([vt_ref[0, :, pl.ds(st, WINDOW)] for st in starts[qi]], axis=1)
        m = jnp.maximum(jnp.max(s_ref[qi], axis=0, keepdims=True), sink)
        p = jnp.exp2(s_ref[qi] - m)
        l = jnp.sum(p, axis=0, keepdims=True) + jnp.exp2(sink - m)
        out_t = jnp.dot(vw, p.astype(BF16), preferred_element_type=F32) * (1.0 / l)
        for g in range(GQA_GROUP):
            o_ref[0, qi * WINDOW:(qi + 1) * WINDOW, g * HEAD_DIM:(g + 1) * HEAD_DIM] = (
                out_t[:, g * WINDOW:(g + 1) * WINDOW].T.astype(BF16))


def _window_attention(qt, k, vt, bias_t, sink_t, *, qb):
    B, _, S, _ = k.shape
    nq = GQA_GROUP * WINDOW
    return pl.pallas_call(
        functools.partial(_window_attn_kernel, qb=qb),
        grid=(B, N_KV_HEADS, S // (qb * WINDOW)),
        in_specs=[
            pl.BlockSpec((1, 1, HEAD_DIM, qb * nq), lambda b, h, n: (b, h, 0, n)),
            pl.BlockSpec((1, 1, S, HEAD_DIM), lambda b, h, n: (b, h, 0, 0)),
            pl.BlockSpec((1, HEAD_DIM, S), lambda b, h, n: (b, h, 0)),
            pl.BlockSpec((1, 3 * WINDOW, nq), lambda b, h, n: (h, 0, 0)),
            pl.BlockSpec((1, 1, nq), lambda b, h, n: (h, 0, 0)),
        ],
        out_specs=pl.BlockSpec((1, qb * WINDOW, GQA_GROUP * HEAD_DIM), lambda b, h, n: (b, n, h)),
        out_shape=jax.ShapeDtypeStruct((B, S, Q_DIM), BF16),
        scratch_shapes=[pltpu.VMEM((qb, 3 * WINDOW, nq), F32)],
        compiler_params=_params(("parallel", "parallel", "parallel")),
        name="window_attention",
    )(qt, k, vt, bias_t, sink_t)


def _layer_norm(y, g, b):
    mu = jnp.mean(y, axis=-1, keepdims=True)
    d = y - mu
    var = jnp.mean(d * d, axis=-1, keepdims=True)
    return d * lax.rsqrt(var + LN_EPS) * g + b


ROW_GROUPS = 4


def _post_attention_kernel(x_ref, o_ref, wo_ref, g1_ref, b1_ref, wg_ref, wu_ref, wd_ref, g2_ref,
                           b2_ref, y_ref):
    tm = x_ref.shape[0]
    rows = [slice(r * tm // ROW_GROUPS, (r + 1) * tm // ROW_GROUPS) for r in range(ROW_GROUPS)]
    hs = [jnp.dot(o_ref[r, :], wo_ref[...], preferred_element_type=F32) for r in rows]
    x1s = [_layer_norm(DEEPNORM_ALPHA * x_ref[r, :] + h, g1_ref[...], b1_ref[...])
           for r, h in zip(rows, hs)]
    mids = []
    for x1 in x1s:
        xb = x1.astype(BF16)
        gate = jnp.dot(xb, wg_ref[...], preferred_element_type=F32)
        up = jnp.dot(xb, wu_ref[...], preferred_element_type=F32)
        mids.append((gate * jax.nn.sigmoid(gate) * up).astype(BF16))
    hs = [jnp.dot(mid, wd_ref[...], preferred_element_type=F32) for mid in mids]
    for r, x1, h in zip(rows, x1s, hs):
        y_ref[r, :] = _layer_norm(DEEPNORM_ALPHA * x1 + h, g2_ref[...], b2_ref[...])


def _post_attention(x, o, wo, g1, b1, wg, wu, wd, g2, b2, *, layer, tm):
    T = x.shape[0]
    vec = _const_spec((1, D_MODEL))

    def stacked(shape):
        return pl.BlockSpec((None,) + shape, lambda t: (layer, 0, 0), pipeline_mode=pl.Buffered(1))

    return pl.pallas_call(
        _post_attention_kernel,
        grid=(T // tm,),
        in_specs=[
            pl.BlockSpec((tm, D_MODEL), lambda i: (i, 0)),
            pl.BlockSpec((tm, Q_DIM), lambda i: (i, 0)),
            _const_spec((Q_DIM, D_MODEL)), vec, vec,
            stacked((D_MODEL, D_FF)),
            stacked((D_MODEL, D_FF)),
            stacked((D_FF, D_MODEL)), vec, vec,
        ],
        out_specs=pl.BlockSpec((tm, D_MODEL), lambda i: (i, 0)),
        out_shape=jax.ShapeDtypeStruct((T, D_MODEL), F32),
        compiler_params=_params(("parallel",)),
        name="post_attention",
    )(x, o, wo, g1, b1, wg, wu, wd, g2, b2)


def _rope_tables_t(seq_len):
    rows_n = seq_len // GRID_W
    rows = jnp.repeat(jnp.arange(rows_n, dtype=F32), GRID_W)
    cols = jnp.tile(jnp.arange(GRID_W, dtype=F32), rows_n)
    inv_freq = ROPE_THETA ** (-jnp.arange(0, AXIS_DIM, 2, dtype=F32) / AXIS_DIM)
    ang_r = (rows[:, None] * inv_freq).T
    ang_c = (cols[:, None] * inv_freq).T
    cos_t = jnp.concatenate([jnp.cos(ang_r)] * 2 + [jnp.cos(ang_c)] * 2, axis=0)
    sin_t = jnp.concatenate([-jnp.sin(ang_r), jnp.sin(ang_r), -jnp.sin(ang_c), jnp.sin(ang_c)],
                            axis=0)
    return cos_t, sin_t


def _rope_partner(g):
    h = AXIS_DIM // 2
    return jnp.concatenate([g[h:2 * h], g[0:h], g[3 * h:4 * h], g[2 * h:3 * h]])


def _t5_bucket(rel):
    nb = N_BUCKETS // 2
    max_exact = nb // 2
    base = (rel > 0).astype(jnp.int32) * nb
    n = jnp.abs(rel)
    nf = jnp.maximum(n, max_exact).astype(F32)
    large = max_exact + (jnp.log(nf / max_exact) / math.log(MAX_DISTANCE / max_exact)
                         * (nb - max_exact)).astype(jnp.int32)
    large = jnp.minimum(large, nb - 1)
    return base + jnp.where(n < max_exact, n, large)


def _window_bias_t(rel_bias_table):
    C = 3 * WINDOW
    n_diag = C + WINDOW - 1
    rel = jnp.arange(n_diag + 1) - (2 * WINDOW - 1)
    bucket = _t5_bucket(rel)[:, None]
    table = rel_bias_table.astype(F32)
    f = sum(jnp.where(bucket == b, table[b], 0.0) for b in range(N_BUCKETS)) * LOG2E
    f = jnp.where((jnp.abs(rel) <= WINDOW)[:, None], f, MASKED)
    skew = jnp.tile(f, (WINDOW, 1))[:WINDOW * n_diag].reshape(WINDOW, n_diag, N_HEADS)
    bias = skew[:, WINDOW - 1:WINDOW - 1 + C]
    bias = bias.transpose(2, 1, 0).reshape(N_KV_HEADS, GQA_GROUP, C, WINDOW)
    return bias.transpose(0, 2, 1, 3).reshape(N_KV_HEADS, C, GQA_GROUP * WINDOW)


def _trunk(x, a_wt, a_q_gain, a_k_gain, a_wo, b_wt, b_sink, b_wo, rel_bias_table,
           ln1_g, ln1_b, wg, wu, wd, ln2_g, ln2_b):
    B, S, _ = x.shape
    T = B * S
    tm_qkv = 1024
    tm_tok = 1024
    cos_t, sin_t = _rope_tables_t(S)
    for i in range(DEPTH):
        j = i // 2
        if i % 2 == 0:
            bq = 512
            gq = a_q_gain[j].astype(F32) * Q_PRESCALE
            gk = a_k_gain[j].astype(F32)
            tables = (gq[:, None] * cos_t, _rope_partner(gq)[:, None] * sin_t,
                      gk[:, None] * cos_t, _rope_partner(gk)[:, None] * sin_t)
            qt, k, vt = _qkv_project(x, a_wt[j], tables, bq=bq, tm=tm_qkv)
            bound = HEAD_DIM * jnp.max(jnp.abs(gq)) * jnp.max(jnp.abs(gk)) * BF16_ROUNDING_SLACK
            attn = functools.partial(_global_attention, bq=bq, bkc=512, n_split=4)
            attn_bounded = functools.partial(_global_attention_bounded, bq=bq, bkc=1024, n_split=4,
                                             unroll=4)
            o = lax.cond(bound <= SAFE_LOG2_SPAN, attn_bounded, attn, qt, k, vt)
            wo = a_wo[j]
        else:
            qt, k, vt = _qkv_project(x, b_wt[j], (), bq=WINDOW, tm=tm_qkv)
            bias_t = _window_bias_t(rel_bias_table)
            sink_t = jnp.repeat(b_sink[j].astype(F32) * LOG2E, WINDOW).reshape(
                N_KV_HEADS, 1, GQA_GROUP * WINDOW)
            o = _window_attention(qt, k, vt, bias_t, sink_t, qb=8)
            wo = b_wo[j]
        x2 = _post_attention(x.reshape(T, D_MODEL), o.reshape(T, Q_DIM), wo,
                             ln1_g[i][None], ln1_b[i][None], wg, wu, wd,
                             ln2_g[i][None], ln2_b[i][None], layer=i, tm=tm_tok)
        x = x2.reshape(B, S, D_MODEL)
    return x


def _prepare_weights(a_w_qkv, a_q_gain, a_k_gain, a_w_o, b_w_qkv, b_sink, b_w_o, rel_bias_table,
                     ln1_g, ln1_b, w_gate, w_up, w_down, ln2_g, ln2_b):
    return (jnp.swapaxes(a_w_qkv, 1, 2).astype(BF16), a_q_gain, a_k_gain, a_w_o.astype(BF16),
            jnp.swapaxes(b_w_qkv, 1, 2).astype(BF16), b_sink, b_w_o.astype(BF16),
            rel_bias_table, ln1_g, ln1_b, w_gate.astype(BF16), w_up.astype(BF16),
            w_down.astype(BF16), ln2_g, ln2_b)


def kernel(x_prompt, x_sample, a_w_qkv, a_q_gain, a_k_gain, a_w_o, b_w_qkv, b_sink, b_w_o,
           rel_bias_table, ln1_g, ln1_b, w_gate, w_up, w_down, ln2_g, ln2_b):
    weights = _prepare_weights(a_w_qkv, a_q_gain, a_k_gain, a_w_o, b_w_qkv, b_sink, b_w_o,
                               rel_bias_table, ln1_g, ln1_b, w_gate, w_up, w_down, ln2_g, ln2_b)
    return (_trunk(x_prompt, *weights), _trunk(x_sample, *weights))
```

```python
import functools
import math

import jax
import jax.numpy as jnp
from jax import lax
from jax.experimental import pallas as pl
from jax.experimental.pallas import tpu as pltpu

D_MODEL = 1024
DEPTH = 2
N_HEADS = 8
N_KV_HEADS = 2
HEAD_DIM = 128
GQA_GROUP = N_HEADS // N_KV_HEADS
Q_DIM = N_HEADS * HEAD_DIM
KV_DIM = N_KV_HEADS * HEAD_DIM
QKV_DIM = Q_DIM + 2 * KV_DIM
D_FF = 2816
GRID_W = 64
AXIS_DIM = HEAD_DIM // 2
ROPE_THETA = 10000.0
WINDOW = 128
N_BUCKETS = 32
MAX_DISTANCE = 128
LN_EPS = 1e-5
RMS_EPS = 1e-6
DEEPNORM_ALPHA = (2.0 * DEPTH) ** 0.25
LOG2E = math.log2(math.e)
Q_PRESCALE = HEAD_DIM ** -0.5 * LOG2E
MASKED = -1e30
SAFE_LOG2_SPAN = 48.0
BF16_ROUNDING_SLACK = 1.02

V7X_VMEM_LIMIT_BYTES = 56 * 1024 * 1024

BF16 = jnp.bfloat16
F32 = jnp.float32


def _params(sem):
    return pltpu.CompilerParams(dimension_semantics=sem, vmem_limit_bytes=V7X_VMEM_LIMIT_BYTES)


def _const_spec(shape):
    nd = len(shape)
    return pl.BlockSpec(shape, lambda *_: (0,) * nd, pipeline_mode=pl.Buffered(1))


HEADS_PER_DOT = 2


def _qkv_kernel(x_ref, wt_ref, *refs, norm_rope, bq):
    if norm_rope:
        cq_ref, sq_ref, ck_ref, sk_ref, qt_ref, k_ref, vt_ref = refs
    else:
        qt_ref, k_ref, vt_ref = refs
    tm = x_ref.shape[1]
    xb = x_ref[0].astype(BF16)

    def norm_rope_slab(y, cos_ref, sin_ref):
        r = lax.rsqrt(jnp.mean(y * y, axis=0, keepdims=True) + RMS_EPS)
        h = AXIS_DIM // 2
        partner = jnp.concatenate([y[h:2 * h], y[0:h], y[3 * h:4 * h], y[2 * h:3 * h]], axis=0)
        return (y * cos_ref[...] + partner * sin_ref[...]) * r

    rows = HEADS_PER_DOT * HEAD_DIM
    for grp in range(QKV_DIM // rows):
        yt = lax.dot_general(wt_ref[grp * rows:(grp + 1) * rows, :], xb, (((1,), (1,)), ((), ())),
                             preferred_element_type=F32)
        for sub in range(HEADS_PER_DOT):
            slab = yt[sub * HEAD_DIM:(sub + 1) * HEAD_DIM]
            head = grp * HEADS_PER_DOT + sub
            if head < N_HEADS:
                slab = norm_rope_slab(slab, cq_ref, sq_ref) if norm_rope else slab * Q_PRESCALE
                slab = slab.astype(BF16)
                kvh, g = divmod(head, GQA_GROUP)
                for j in range(tm // bq):
                    col = (j * GQA_GROUP + g) * bq
                    qt_ref[0, kvh, :, col:col + bq] = slab[:, j * bq:(j + 1) * bq]
            elif head < N_HEADS + N_KV_HEADS:
                kvh = head - N_HEADS
                if norm_rope:
                    slab = norm_rope_slab(slab, ck_ref, sk_ref)
                k_ref[0, kvh] = slab.T.astype(BF16)
            else:
                kvh = head - N_HEADS - N_KV_HEADS
                vt_ref[0, kvh * HEAD_DIM:(kvh + 1) * HEAD_DIM, :] = slab.astype(BF16)


def _qkv_project(x, wt, tables, *, bq, tm):
    B, S, _ = x.shape
    kern = functools.partial(_qkv_kernel, norm_rope=bool(tables), bq=bq)
    return pl.pallas_call(
        kern,
        grid=(B, S // tm),
        in_specs=[
            pl.BlockSpec((1, tm, D_MODEL), lambda b, i: (b, i, 0)),
            _const_spec((QKV_DIM, D_MODEL)),
        ] + [pl.BlockSpec((HEAD_DIM, tm), lambda b, i: (0, i)) for _ in tables],
        out_specs=[
            pl.BlockSpec((1, N_KV_HEADS, HEAD_DIM, GQA_GROUP * tm), lambda b, i: (b, 0, 0, i)),
            pl.BlockSpec((1, N_KV_HEADS, tm, HEAD_DIM), lambda b, i: (b, 0, i, 0)),
            pl.BlockSpec((1, KV_DIM, tm), lambda b, i: (b, 0, i)),
        ],
        out_shape=[
            jax.ShapeDtypeStruct((B, N_KV_HEADS, HEAD_DIM, GQA_GROUP * S), BF16),
            jax.ShapeDtypeStruct((B, N_KV_HEADS, S, HEAD_DIM), BF16),
            jax.ShapeDtypeStruct((B, KV_DIM, S), BF16),
        ],
        compiler_params=_params(("parallel", "parallel")),
        name="qkv_project",
    )(x, wt, *tables)


def _store_heads(o_ref, out_t, bq):
    for g in range(GQA_GROUP):
        o_ref[0, :, g * HEAD_DIM:(g + 1) * HEAD_DIM] = out_t[:, g * bq:(g + 1) * bq].T.astype(BF16)


def _global_attn_kernel(qt_ref, k_ref, vt_ref, o_ref, acc_ref, *, bq, bkc, n_split):
    S = k_ref.shape[2]
    nq = GQA_GROUP * bq
    w = nq // n_split
    acc_ref[...] = jnp.zeros_like(acc_ref)

    def body(c, carry):
        ms, ls = carry
        start = pl.multiple_of(c * bkc, bkc)
        k = k_ref[0, 0, pl.ds(start, bkc), :]
        vt = vt_ref[0, :, pl.ds(start, bkc)]
        new_ms, new_ls = [], []
        for h in range(n_split):
            q = qt_ref[0, 0, :, h * w:(h + 1) * w]
            s = jnp.dot(k, q, preferred_element_type=F32)
            m_new = jnp.maximum(ms[h], jnp.max(s, axis=0, keepdims=True))
            alpha = jnp.exp2(ms[h] - m_new)
            p = jnp.exp2(s - m_new)
            new_ls.append(alpha * ls[h] + jnp.sum(p, axis=0, keepdims=True))
            new_ms.append(m_new)
            pv = jnp.dot(vt, p.astype(BF16), preferred_element_type=F32)
            acc_ref[:, h * w:(h + 1) * w] = alpha * acc_ref[:, h * w:(h + 1) * w] + pv
        return tuple(new_ms), tuple(new_ls)

    init = (tuple(jnp.full((1, w), MASKED, F32) for _ in range(n_split)),
            tuple(jnp.zeros((1, w), F32) for _ in range(n_split)))
    _, ls = lax.fori_loop(0, S // bkc, body, init)
    l = jnp.concatenate(ls, axis=1)
    _store_heads(o_ref, acc_ref[...] * (1.0 / l), bq)


def _global_attn_bounded_kernel(qt_ref, k_ref, vt_ref, o_ref, acc_ref, *, bq, bkc, n_split,
                                unroll):
    S = k_ref.shape[2]
    nq = GQA_GROUP * bq
    w = nq // n_split
    acc_ref[...] = jnp.zeros_like(acc_ref)

    def body(c, ls):
        start = pl.multiple_of(c * bkc, bkc)
        k = k_ref[0, 0, pl.ds(start, bkc), :]
        vt = vt_ref[0, :, pl.ds(start, bkc)]
        ss = [jnp.dot(k, qt_ref[0, 0, :, h * w:(h + 1) * w], preferred_element_type=F32)
              for h in range(n_split)]
        new_ls = []
        for h in range(n_split):
            cols = slice(h * w, (h + 1) * w)
            p = jnp.exp2(ss[h])
            new_ls.append(ls[h] + jnp.sum(p.reshape(bkc // 8, 8, w), axis=0))
            acc_ref[:, cols] += jnp.dot(vt, p.astype(BF16), preferred_element_type=F32)
        return tuple(new_ls)

    ls = lax.fori_loop(0, S // bkc, body, tuple(jnp.zeros((8, w), F32) for _ in range(n_split)),
                       unroll=unroll)
    l = jnp.concatenate([jnp.sum(x, axis=0, keepdims=True) for x in ls], axis=1)
    _store_heads(o_ref, acc_ref[...] * (1.0 / l), bq)


def _global_attention_call(kern, name, qt, k, vt, *, bq):
    B, _, S, _ = k.shape
    nq = GQA_GROUP * bq
    return pl.pallas_call(
        kern,
        grid=(B, N_KV_HEADS, S // bq),
        in_specs=[
            pl.BlockSpec((1, 1, HEAD_DIM, nq), lambda b, h, i: (b, h, 0, i)),
            pl.BlockSpec((1, 1, S, HEAD_DIM), lambda b, h, i: (b, h, 0, 0)),
            pl.BlockSpec((1, HEAD_DIM, S), lambda b, h, i: (b, h, 0)),
        ],
        out_specs=pl.BlockSpec((1, bq, GQA_GROUP * HEAD_DIM), lambda b, h, i: (b, i, h)),
        out_shape=jax.ShapeDtypeStruct((B, S, Q_DIM), BF16),
        scratch_shapes=[pltpu.VMEM((HEAD_DIM, nq), F32)],
        compiler_params=_params(("parallel", "parallel", "arbitrary")),
        name=name,
    )(qt, k, vt)


def _global_attention_bounded(qt, k, vt, *, bq, bkc, n_split, unroll):
    kern = functools.partial(_global_attn_bounded_kernel, bq=bq, bkc=bkc, n_split=n_split,
                             unroll=unroll)
    return _global_attention_call(kern, "global_attention_bounded", qt, k, vt, bq=bq)


def _global_attention(qt, k, vt, *, bq, bkc, n_split):
    kern = functools.partial(_global_attn_kernel, bq=bq, bkc=bkc, n_split=n_split)
    return _global_attention_call(kern, "global_attention", qt, k, vt, bq=bq)


def _window_attn_kernel(qt_ref, k_ref, vt_ref, bias_ref, sink_ref, o_ref, s_ref, *, qb):
    step = pl.program_id(2)
    last_step = pl.num_programs(2) - 1
    nb = k_ref.shape[2] // WINDOW
    nq = GQA_GROUP * WINDOW
    sink = sink_ref[0]

    def block_start(n):
        return pl.multiple_of(n * WINDOW, WINDOW)

    starts = [[block_start(jnp.maximum(step * qb + qi - 1, 0)), block_start(step * qb + qi),
               block_start(jnp.minimum(step * qb + qi + 1, nb - 1))] for qi in range(qb)]
    for qi in range(qb):
        kw = jnp.concatenate([k_ref[0, 0, pl.ds(st, WINDOW), :] for st in starts[qi]], axis=0)
        s = jnp.dot(kw, qt_ref[0, 0, :, qi * nq:(qi + 1) * nq],
                    preferred_element_type=F32) + bias_ref[0]
        if qi == 0:
            s = jnp.concatenate(
                [jnp.where(step == 0, MASKED, s[:WINDOW]), s[WINDOW:]], axis=0)
        if qi == qb - 1:
            s = jnp.concatenate(
                [s[:2 * WINDOW], jnp.where(step == last_step, MASKED, s[2 * WINDOW:])], axis=0)
        s_ref[qi] = s
    for qi in range(qb):
        vw = jnp.concatenate([vt_ref[0, :, pl.ds(st, WINDOW)] for st in starts[qi]], axis=1)
        m = jnp.maximum(jnp.max(s_ref[qi], axis=0, keepdims=True), sink)
        p = jnp.exp2(s_ref[qi] - m)
        l = jnp.sum(p, axis=0, keepdims=True) + jnp.exp2(sink - m)
        out_t = jnp.dot(vw, p.astype(BF16), preferred_element_type=F32) * (1.0 / l)
        for g in range(GQA_GROUP):
            o_ref[0, qi * WINDOW:(qi + 1) * WINDOW, g * HEAD_DIM:(g + 1) * HEAD_DIM] = (
                out_t[:, g * WINDOW:(g + 1) * WINDOW].T.astype(BF16))


def _window_attention(qt, k, vt, bias_t, sink_t, *, qb):
    B, _, S, _ = k.shape
    nq = GQA_GROUP * WINDOW
    return pl.pallas_call(
        functools.partial(_window_attn_kernel, qb=qb),
        grid=(B, N_KV_HEADS, S // (qb * WINDOW)),
        in_specs=[
            pl.BlockSpec((1, 1, HEAD_DIM, qb * nq), lambda b, h, n: (b, h, 0, n)),
            pl.BlockSpec((1, 1, S, HEAD_DIM), lambda b, h, n: (b, h, 0, 0)),
            pl.BlockSpec((1, HEAD_DIM, S), lambda b, h, n: (b, h, 0)),
            pl.BlockSpec((1, 3 * WINDOW, nq), lambda b, h, n: (h, 0, 0)),
            pl.BlockSpec((1, 1, nq), lambda b, h, n: (h, 0, 0)),
        ],
        out_specs=pl.BlockSpec((1, qb * WINDOW, GQA_GROUP * HEAD_DIM), lambda b, h, n: (b, n, h)),
        out_shape=jax.ShapeDtypeStruct((B, S, Q_DIM), BF16),
        scratch_shapes=[pltpu.VMEM((qb, 3 * WINDOW, nq), F32)],
        compiler_params=_params(("parallel", "parallel", "parallel")),
        name="window_attention",
    )(qt, k, vt, bias_t, sink_t)


def _layer_norm(y, g, b):
    mu = jnp.mean(y, axis=-1, keepdims=True)
    d = y - mu
    var = jnp.mean(d * d, axis=-1, keepdims=True)
    return d * lax.rsqrt(var + LN_EPS) * g + b


ROW_GROUPS = 4


def _post_attention_kernel(x_ref, o_ref, wo_ref, g1_ref, b1_ref, wg_ref, wu_ref, wd_ref, g2_ref,
                           b2_ref, y_ref):
    tm = x_ref.shape[0]
    rows = [slice(r * tm // ROW_GROUPS, (r + 1) * tm // ROW_GROUPS) for r in range(ROW_GROUPS)]
    hs = [jnp.dot(o_ref[r, :], wo_ref[...], preferred_element_type=F32) for r in rows]
    x1s = [_layer_norm(DEEPNORM_ALPHA * x_ref[r, :] + h, g1_ref[...], b1_ref[...])
           for r, h in zip(rows, hs)]
    mids = []
    for x1 in x1s:
        xb = x1.astype(BF16)
        gate = jnp.dot(xb, wg_ref[...], preferred_element_type=F32)
        up = jnp.dot(xb, wu_ref[...], preferred_element_type=F32)
        mids.append((gate * jax.nn.sigmoid(gate) * up).astype(BF16))
    hs = [jnp.dot(mid, wd_ref[...], preferred_element_type=F32) for mid in mids]
    for r, x1, h in zip(rows, x1s, hs):
        y_ref[r, :] = _layer_norm(DEEPNORM_ALPHA * x1 + h, g2_ref[...], b2_ref[...])


def _post_attention(x, o, wo, g1, b1, wg, wu, wd, g2, b2, *, layer, tm):
    T = x.shape[0]
    vec = _const_spec((1, D_MODEL))

    def stacked(shape):
        return pl.BlockSpec((None,) + shape, lambda t: (layer, 0, 0), pipeline_mode=pl.Buffered(1))

    return pl.pallas_call(
        _post_attention_kernel,
        grid=(T // tm,),
        in_specs=[
            pl.BlockSpec((tm, D_MODEL), lambda i: (i, 0)),
            pl.BlockSpec((tm, Q_DIM), lambda i: (i, 0)),
            _const_spec((Q_DIM, D_MODEL)), vec, vec,
            stacked((D_MODEL, D_FF)),
            stacked((D_MODEL, D_FF)),
            stacked((D_FF, D_MODEL)), vec, vec,
        ],
        out_specs=pl.BlockSpec((tm, D_MODEL), lambda i: (i, 0)),
        out_shape=jax.ShapeDtypeStruct((T, D_MODEL), F32),
        compiler_params=_params(("parallel",)),
        name="post_attention",
    )(x, o, wo, g1, b1, wg, wu, wd, g2, b2)


def _rope_tables_t(seq_len):
    rows_n = seq_len // GRID_W
    rows = jnp.repeat(jnp.arange(rows_n, dtype=F32), GRID_W)
    cols = jnp.tile(jnp.arange(GRID_W, dtype=F32), rows_n)
    inv_freq = ROPE_THETA ** (-jnp.arange(0, AXIS_DIM, 2, dtype=F32) / AXIS_DIM)
    ang_r = (rows[:, None] * inv_freq).T
    ang_c = (cols[:, None] * inv_freq).T
    cos_t = jnp.concatenate([jnp.cos(ang_r)] * 2 + [jnp.cos(ang_c)] * 2, axis=0)
    sin_t = jnp.concatenate([-jnp.sin(ang_r), jnp.sin(ang_r), -jnp.sin(ang_c), jnp.sin(ang_c)],
                            axis=0)
    return cos_t, sin_t


def _rope_partner(g):
    h = AXIS_DIM // 2
    return jnp.concatenate([g[h:2 * h], g[0:h], g[3 * h:4 * h], g[2 * h:3 * h]])


def _t5_bucket(rel):
    nb = N_BUCKETS // 2
    max_exact = nb // 2
    base = (rel > 0).astype(jnp.int32) * nb
    n = jnp.abs(rel)
    nf = jnp.maximum(n, max_exact).astype(F32)
    large = max_exact + (jnp.log(nf / max_exact) / math.log(MAX_DISTANCE / max_exact)
                         * (nb - max_exact)).astype(jnp.int32)
    large = jnp.minimum(large, nb - 1)
    return base + jnp.where(n < max_exact, n, large)


def _window_bias_t(rel_bias_table):
    C = 3 * WINDOW
    n_diag = C + WINDOW - 1
    rel = jnp.arange(n_diag + 1) - (2 * WINDOW - 1)
    bucket = _t5_bucket(rel)[:, None]
    table = rel_bias_table.astype(F32)
    f = sum(jnp.where(bucket == b, table[b], 0.0) for b in range(N_BUCKETS)) * LOG2E
    f = jnp.where((jnp.abs(rel) <= WINDOW)[:, None], f, MASKED)
    skew = jnp.tile(f, (WINDOW, 1))[:WINDOW * n_diag].reshape(WINDOW, n_diag, N_HEADS)
    bias = skew[:, WINDOW - 1:WINDOW - 1 + C]
    bias = bias.transpose(2, 1, 0).reshape(N_KV_HEADS, GQA_GROUP, C, WINDOW)
    return bias.transpose(0, 2, 1, 3).reshape(N_KV_HEADS, C, GQA_GROUP * WINDOW)


def _trunk(x, a_wt, a_q_gain, a_k_gain, a_wo, b_wt, b_sink, b_wo, rel_bias_table,
           ln1_g, ln1_b, wg, wu, wd, ln2_g, ln2_b):
    B, S, _ = x.shape
    T = B * S
    tm_qkv = min(2048, S)
    tm_tok = min(1024, T)
    bq = min(1024, S)
    bkc = min(1024, S)
    qb = min(16, S // WINDOW)
    cos_t, sin_t = _rope_tables_t(S)
    for i in range(DEPTH):
        j = i // 2
        if i % 2 == 0:
            gq = a_q_gain[j].astype(F32) * Q_PRESCALE
            gk = a_k_gain[j].astype(F32)
            tables = (gq[:, None] * cos_t, _rope_partner(gq)[:, None] * sin_t,
                      gk[:, None] * cos_t, _rope_partner(gk)[:, None] * sin_t)
            qt, k, vt = _qkv_project(x, a_wt[j], tables, bq=bq, tm=tm_qkv)
            bound = HEAD_DIM * jnp.max(jnp.abs(gq)) * jnp.max(jnp.abs(gk)) * BF16_ROUNDING_SLACK
            n_split = GQA_GROUP * bq // 512
            attn = functools.partial(_global_attention, bq=bq, bkc=512, n_split=n_split)
            attn_bounded = functools.partial(_global_attention_bounded, bq=bq, bkc=bkc,
                                             n_split=n_split, unroll=2)
            o = lax.cond(bound <= SAFE_LOG2_SPAN, attn_bounded, attn, qt, k, vt)
            wo = a_wo[j]
        else:
            qt, k, vt = _qkv_project(x, b_wt[j], (), bq=WINDOW, tm=tm_qkv)
            bias_t = _window_bias_t(rel_bias_table)
            sink_t = jnp.repeat(b_sink[j].astype(F32) * LOG2E, WINDOW).reshape(
                N_KV_HEADS, 1, GQA_GROUP * WINDOW)
            o = _window_attention(qt, k, vt, bias_t, sink_t, qb=qb)
            wo = b_wo[j]
        x2 = _post_attention(x.reshape(T, D_MODEL), o.reshape(T, Q_DIM), wo,
                             ln1_g[i][None], ln1_b[i][None], wg, wu, wd,
                             ln2_g[i][None], ln2_b[i][None], layer=i, tm=tm_tok)
        x = x2.reshape(B, S, D_MODEL)
    return x


def _prepare_weights(a_w_qkv, a_q_gain, a_k_gain, a_w_o, b_w_qkv, b_sink, b_w_o, rel_bias_table,
                     ln1_g, ln1_b, w_gate, w_up, w_down, ln2_g, ln2_b):
    return (jnp.swapaxes(a_w_qkv, 1, 2).astype(BF16), a_q_gain, a_k_gain, a_w_o.astype(BF16),
            jnp.swapaxes(b_w_qkv, 1, 2).astype(BF16), b_sink, b_w_o.astype(BF16),
            rel_bias_table, ln1_g, ln1_b, w_gate.astype(BF16), w_up.astype(BF16),
            w_down.astype(BF16), ln2_g, ln2_b)


def kernel(x_prompt, x_sample, a_w_qkv, a_q_gain, a_k_gain, a_w_o, b_w_qkv, b_sink, b_w_o,
           rel_bias_table, ln1_g, ln1_b, w_gate, w_up, w_down, ln2_g, ln2_b):
    weights = _prepare_weights(a_w_qkv, a_q_gain, a_k_gain, a_w_o, b_w_qkv, b_sink, b_w_o,
                               rel_bias_table, ln1_g, ln1_b, w_gate, w_up, w_down, ln2_g, ln2_b)
    return (_trunk(x_prompt, *weights), _trunk(x_sample, *weights))
```

```python
import functools
import math

import jax
import jax.numpy as jnp
from jax import lax
from jax.experimental import pallas as pl
from jax.experimental.pallas import tpu as pltpu

D_MODEL = 1024
DEPTH = 2
N_HEADS = 8
N_KV_HEADS = 2
HEAD_DIM = 128
GQA_GROUP = N_HEADS // N_KV_HEADS
Q_DIM = N_HEADS * HEAD_DIM
KV_DIM = N_KV_HEADS * HEAD_DIM
QKV_DIM = Q_DIM + 2 * KV_DIM
D_FF = 2816
GRID_W = 64
AXIS_DIM = HEAD_DIM // 2
ROPE_THETA = 10000.0
WINDOW = 128
N_BUCKETS = 32
MAX_DISTANCE = 128
LN_EPS = 1e-5
RMS_EPS = 1e-6
DEEPNORM_ALPHA = (2.0 * DEPTH) ** 0.25
LOG2E = math.log2(math.e)
Q_PRESCALE = HEAD_DIM ** -0.5 * LOG2E
MASKED = -1e30
SAFE_LOG2_SPAN = 48.0
BF16_ROUNDING_SLACK = 1.02

V7X_VMEM_LIMIT_BYTES = 56 * 1024 * 1024
V7X_F32_TILE = (8, 128)

BF16 = jnp.bfloat16
F32 = jnp.float32


def _params(sem):
    return pltpu.CompilerParams(dimension_semantics=sem, vmem_limit_bytes=V7X_VMEM_LIMIT_BYTES)


def _const_spec(shape):
    nd = len(shape)
    return pl.BlockSpec(shape, lambda *_: (0,) * nd, pipeline_mode=pl.Buffered(1))


HEADS_PER_DOT = 2


def _qkv_kernel(x_ref, wt_ref, *refs, norm_rope, bq):
    if norm_rope:
        cq_ref, sq_ref, ck_ref, sk_ref, qt_ref, k_ref, vt_ref = refs
    else:
        qt_ref, k_ref, vt_ref, norm_ref = refs
    tm = x_ref.shape[1]
    xb = x_ref[0].astype(BF16)
    max_sq = {"q": None, "k": None}

    def track_norm(name, slab):
        sq = jnp.sum(slab * slab, axis=0, keepdims=True)
        max_sq[name] = sq if max_sq[name] is None else jnp.maximum(max_sq[name], sq)

    def norm_rope_slab(y, cos_ref, sin_ref):
        r = lax.rsqrt(jnp.mean(y * y, axis=0, keepdims=True) + RMS_EPS)
        h = AXIS_DIM // 2
        partner = jnp.concatenate([y[h:2 * h], y[0:h], y[3 * h:4 * h], y[2 * h:3 * h]], axis=0)
        return (y * cos_ref[...] + partner * sin_ref[...]) * r

    rows = HEADS_PER_DOT * HEAD_DIM
    for grp in range(QKV_DIM // rows):
        yt = lax.dot_general(wt_ref[grp * rows:(grp + 1) * rows, :], xb, (((1,), (1,)), ((), ())),
                             preferred_element_type=F32)
        for sub in range(HEADS_PER_DOT):
            slab = yt[sub * HEAD_DIM:(sub + 1) * HEAD_DIM]
            head = grp * HEADS_PER_DOT + sub
            if head < N_HEADS:
                if norm_rope:
                    slab = norm_rope_slab(slab, cq_ref, sq_ref)
                else:
                    slab = slab * Q_PRESCALE
                    track_norm("q", slab)
                slab = slab.astype(BF16)
                kvh, g = divmod(head, GQA_GROUP)
                for j in range(tm // bq):
                    col = (j * GQA_GROUP + g) * bq
                    qt_ref[0, kvh, :, col:col + bq] = slab[:, j * bq:(j + 1) * bq]
            elif head < N_HEADS + N_KV_HEADS:
                kvh = head - N_HEADS
                if norm_rope:
                    slab = norm_rope_slab(slab, ck_ref, sk_ref)
                else:
                    track_norm("k", slab)
                k_ref[0, kvh] = slab.T.astype(BF16)
            else:
                kvh = head - N_HEADS - N_KV_HEADS
                vt_ref[0, kvh * HEAD_DIM:(kvh + 1) * HEAD_DIM, :] = slab.astype(BF16)
    if not norm_rope:
        row = lax.broadcasted_iota(jnp.int32, norm_ref.shape[2:], 0)
        norm_ref[0, 0] = jnp.where(row == 0, jnp.max(max_sq["q"], axis=1, keepdims=True),
                                   jnp.max(max_sq["k"], axis=1, keepdims=True))


def _qkv_project(x, wt, tables, *, bq, tm):
    B, S, _ = x.shape
    kern = functools.partial(_qkv_kernel, norm_rope=bool(tables), bq=bq)
    out_specs = [
        pl.BlockSpec((1, N_KV_HEADS, HEAD_DIM, GQA_GROUP * tm), lambda b, i: (b, 0, 0, i)),
        pl.BlockSpec((1, N_KV_HEADS, tm, HEAD_DIM), lambda b, i: (b, 0, i, 0)),
        pl.BlockSpec((1, KV_DIM, tm), lambda b, i: (b, 0, i)),
    ]
    out_shape = [
        jax.ShapeDtypeStruct((B, N_KV_HEADS, HEAD_DIM, GQA_GROUP * S), BF16),
        jax.ShapeDtypeStruct((B, N_KV_HEADS, S, HEAD_DIM), BF16),
        jax.ShapeDtypeStruct((B, KV_DIM, S), BF16),
    ]
    if not tables:
        out_specs.append(pl.BlockSpec((1, 1) + V7X_F32_TILE, lambda b, i: (b, i, 0, 0)))
        out_shape.append(jax.ShapeDtypeStruct((B, S // tm) + V7X_F32_TILE, F32))
    return pl.pallas_call(
        kern,
        grid=(B, S // tm),
        in_specs=[
            pl.BlockSpec((1, tm, D_MODEL), lambda b, i: (b, i, 0)),
            _const_spec((QKV_DIM, D_MODEL)),
        ] + [pl.BlockSpec((HEAD_DIM, tm), lambda b, i: (0, i)) for _ in tables],
        out_specs=out_specs,
        out_shape=out_shape,
        compiler_params=_params(("parallel", "parallel")),
        name="qkv_project",
    )(x, wt, *tables)


def _store_heads(o_ref, out_t, bq):
    for g in range(GQA_GROUP):
        o_ref[0, :, g * HEAD_DIM:(g + 1) * HEAD_DIM] = out_t[:, g * bq:(g + 1) * bq].T.astype(BF16)


def _global_attn_kernel(qt_ref, k_ref, vt_ref, o_ref, acc_ref, *, bq, bkc, n_split):
    S = k_ref.shape[2]
    nq = GQA_GROUP * bq
    w = nq // n_split
    acc_ref[...] = jnp.zeros_like(acc_ref)

    def body(c, carry):
        ms, ls = carry
        start = pl.multiple_of(c * bkc, bkc)
        k = k_ref[0, 0, pl.ds(start, bkc), :]
        vt = vt_ref[0, :, pl.ds(start, bkc)]
        new_ms, new_ls = [], []
        for h in range(n_split):
            q = qt_ref[0, 0, :, h * w:(h + 1) * w]
            s = jnp.dot(k, q, preferred_element_type=F32)
            m_new = jnp.maximum(ms[h], jnp.max(s, axis=0, keepdims=True))
            alpha = jnp.exp2(ms[h] - m_new)
            p = jnp.exp2(s - m_new)
            new_ls.append(alpha * ls[h] + jnp.sum(p, axis=0, keepdims=True))
            new_ms.append(m_new)
            pv = jnp.dot(vt, p.astype(BF16), preferred_element_type=F32)
            acc_ref[:, h * w:(h + 1) * w] = alpha * acc_ref[:, h * w:(h + 1) * w] + pv
        return tuple(new_ms), tuple(new_ls)

    init = (tuple(jnp.full((1, w), MASKED, F32) for _ in range(n_split)),
            tuple(jnp.zeros((1, w), F32) for _ in range(n_split)))
    _, ls = lax.fori_loop(0, S // bkc, body, init)
    l = jnp.concatenate(ls, axis=1)
    _store_heads(o_ref, acc_ref[...] * (1.0 / l), bq)


def _global_attn_bounded_kernel(qt_ref, k_ref, vt_ref, o_ref, acc_ref, *, bq, bkc, n_split,
                                unroll):
    S = k_ref.shape[2]
    nq = GQA_GROUP * bq
    w = nq // n_split
    acc_ref[...] = jnp.zeros_like(acc_ref)

    def body(c, ls):
        start = pl.multiple_of(c * bkc, bkc)
        k = k_ref[0, 0, pl.ds(start, bkc), :]
        vt = vt_ref[0, :, pl.ds(start, bkc)]
        ss = [jnp.dot(k, qt_ref[0, 0, :, h * w:(h + 1) * w], preferred_element_type=F32)
              for h in range(n_split)]
        new_ls = []
        for h in range(n_split):
            cols = slice(h * w, (h + 1) * w)
            p = jnp.exp2(ss[h])
            new_ls.append(ls[h] + jnp.sum(p.reshape(bkc // 8, 8, w), axis=0))
            acc_ref[:, cols] += jnp.dot(vt, p.astype(BF16), preferred_element_type=F32)
        return tuple(new_ls)

    ls = lax.fori_loop(0, S // bkc, body, tuple(jnp.zeros((8, w), F32) for _ in range(n_split)),
                       unroll=unroll)
    l = jnp.concatenate([jnp.sum(x, axis=0, keepdims=True) for x in ls], axis=1)
    _store_heads(o_ref, acc_ref[...] * (1.0 / l), bq)


def _global_attention_call(kern, name, qt, k, vt, *, bq):
    B, _, S, _ = k.shape
    nq = GQA_GROUP * bq
    return pl.pallas_call(
        kern,
        grid=(B, N_KV_HEADS, S // bq),
        in_specs=[
            pl.BlockSpec((1, 1, HEAD_DIM, nq), lambda b, h, i: (b, h, 0, i)),
            pl.BlockSpec((1, 1, S, HEAD_DIM), lambda b, h, i: (b, h, 0, 0)),
            pl.BlockSpec((1, HEAD_DIM, S), lambda b, h, i: (b, h, 0)),
        ],
        out_specs=pl.BlockSpec((1, bq, GQA_GROUP * HEAD_DIM), lambda b, h, i: (b, i, h)),
        out_shape=jax.ShapeDtypeStruct((B, S, Q_DIM), BF16),
        scratch_shapes=[pltpu.VMEM((HEAD_DIM, nq), F32)],
        compiler_params=_params(("parallel", "parallel", "arbitrary")),
        name=name,
    )(qt, k, vt)


def _global_attention_bounded(qt, k, vt, *, bq, bkc, n_split, unroll):
    kern = functools.partial(_global_attn_bounded_kernel, bq=bq, bkc=bkc, n_split=n_split,
                             unroll=unroll)
    return _global_attention_call(kern, "global_attention_bounded", qt, k, vt, bq=bq)


def _global_attention(qt, k, vt, *, bq, bkc, n_split):
    kern = functools.partial(_global_attn_kernel, bq=bq, bkc=bkc, n_split=n_split)
    return _global_attention_call(kern, "global_attention", qt, k, vt, bq=bq)


def _window_attn_kernel(qt_ref, k_ref, vt_ref, bias_ref, sink_ref, o_ref, s_ref, *, qb, bounded):
    step = pl.program_id(2)
    last_step = pl.num_programs(2) - 1
    nb = k_ref.shape[2] // WINDOW
    nq = GQA_GROUP * WINDOW
    sink = sink_ref[0]

    def block_start(n):
        return pl.multiple_of(n * WINDOW, WINDOW)

    starts = [[block_start(jnp.maximum(step * qb + qi - 1, 0)), block_start(step * qb + qi),
               block_start(jnp.minimum(step * qb + qi + 1, nb - 1))] for qi in range(qb)]
    for qi in range(qb):
        kw = jnp.concatenate([k_ref[0, 0, pl.ds(st, WINDOW), :] for st in starts[qi]], axis=0)
        s = jnp.dot(kw, qt_ref[0, 0, :, qi * nq:(qi + 1) * nq],
                    preferred_element_type=F32) + bias_ref[0]
        if qi == 0:
            s = jnp.concatenate(
                [jnp.where(step == 0, MASKED, s[:WINDOW]), s[WINDOW:]], axis=0)
        if qi == qb - 1:
            s = jnp.concatenate(
                [s[:2 * WINDOW], jnp.where(step == last_step, MASKED, s[2 * WINDOW:])], axis=0)
        s_ref[qi] = s
    for qi in range(qb):
        vw = jnp.concatenate([vt_ref[0, :, pl.ds(st, WINDOW)] for st in starts[qi]], axis=1)
        if bounded:
            p = jnp.exp2(s_ref[qi])
            l = jnp.sum(p, axis=0, keepdims=True) + jnp.exp2(sink)
        else:
            m = jnp.maximum(jnp.max(s_ref[qi], axis=0, keepdims=True), sink)
            p = jnp.exp2(s_ref[qi] - m)
            l = jnp.sum(p, axis=0, keepdims=True) + jnp.exp2(sink - m)
        out_t = jnp.dot(vw, p.astype(BF16), preferred_element_type=F32) * (1.0 / l)
        for g in range(GQA_GROUP):
            o_ref[0, qi * WINDOW:(qi + 1) * WINDOW, g * HEAD_DIM:(g + 1) * HEAD_DIM] = (
                out_t[:, g * WINDOW:(g + 1) * WINDOW].T.astype(BF16))


def _window_attention(qt, k, vt, bias_t, sink_t, *, qb, bounded):
    B, _, S, _ = k.shape
    nq = GQA_GROUP * WINDOW
    return pl.pallas_call(
        functools.partial(_window_attn_kernel, qb=qb, bounded=bounded),
        grid=(B, N_KV_HEADS, S // (qb * WINDOW)),
        in_specs=[
            pl.BlockSpec((1, 1, HEAD_DIM, qb * nq), lambda b, h, n: (b, h, 0, n)),
            pl.BlockSpec((1, 1, S, HEAD_DIM), lambda b, h, n: (b, h, 0, 0)),
            pl.BlockSpec((1, HEAD_DIM, S), lambda b, h, n: (b, h, 0)),
            pl.BlockSpec((1, 3 * WINDOW, nq), lambda b, h, n: (h, 0, 0)),
            pl.BlockSpec((1, 1, nq), lambda b, h, n: (h, 0, 0)),
        ],
        out_specs=pl.BlockSpec((1, qb * WINDOW, GQA_GROUP * HEAD_DIM), lambda b, h, n: (b, n, h)),
        out_shape=jax.ShapeDtypeStruct((B, S, Q_DIM), BF16),
        scratch_shapes=[pltpu.VMEM((qb, 3 * WINDOW, nq), F32)],
        compiler_params=_params(("parallel", "parallel", "parallel")),
        name="window_attention",
    )(qt, k, vt, bias_t, sink_t)


def _layer_norm(y, g, b):
    mu = jnp.mean(y, axis=-1, keepdims=True)
    d = y - mu
    var = jnp.mean(d * d, axis=-1, keepdims=True)
    return d * lax.rsqrt(var + LN_EPS) * g + b


ROW_GROUPS = 4


def _post_attention_kernel(x_ref, o_ref, wo_ref, g1_ref, b1_ref, wg_ref, wu_ref, wd_ref, g2_ref,
                           b2_ref, y_ref):
    tm = x_ref.shape[0]
    rows = [slice(r * tm // ROW_GROUPS, (r + 1) * tm // ROW_GROUPS) for r in range(ROW_GROUPS)]
    hs = [jnp.dot(o_ref[r, :], wo_ref[...], preferred_element_type=F32) for r in rows]
    x1s = [_layer_norm(DEEPNORM_ALPHA * x_ref[r, :] + h, g1_ref[...], b1_ref[...])
           for r, h in zip(rows, hs)]
    mids = []
    for x1 in x1s:
        xb = x1.astype(BF16)
        gate = jnp.dot(xb, wg_ref[...], preferred_element_type=F32)
        up = jnp.dot(xb, wu_ref[...], preferred_element_type=F32)
        mids.append((gate * jax.nn.sigmoid(gate) * up).astype(BF16))
    hs = [jnp.dot(mid, wd_ref[...], preferred_element_type=F32) for mid in mids]
    for r, x1, h in zip(rows, x1s, hs):
        y_ref[r, :] = _layer_norm(DEEPNORM_ALPHA * x1 + h, g2_ref[...], b2_ref[...])


def _post_attention(x, o, wo, g1, b1, wg, wu, wd, g2, b2, *, layer, tm):
    T = x.shape[0]
    vec = _const_spec((1, D_MODEL))

    def stacked(shape):
        return pl.BlockSpec((None,) + shape, lambda t: (layer, 0, 0), pipeline_mode=pl.Buffered(1))

    return pl.pallas_call(
        _post_attention_kernel,
        grid=(T // tm,),
        in_specs=[
            pl.BlockSpec((tm, D_MODEL), lambda i: (i, 0)),
            pl.BlockSpec((tm, Q_DIM), lambda i: (i, 0)),
            _const_spec((Q_DIM, D_MODEL)), vec, vec,
            stacked((D_MODEL, D_FF)),
            stacked((D_MODEL, D_FF)),
            stacked((D_FF, D_MODEL)), vec, vec,
        ],
        out_specs=pl.BlockSpec((tm, D_MODEL), lambda i: (i, 0)),
        out_shape=jax.ShapeDtypeStruct((T, D_MODEL), F32),
        compiler_params=_params(("parallel",)),
        name="post_attention",
    )(x, o, wo, g1, b1, wg, wu, wd, g2, b2)


def _rope_tables_t(seq_len):
    rows_n = seq_len // GRID_W
    rows = jnp.repeat(jnp.arange(rows_n, dtype=F32), GRID_W)
    cols = jnp.tile(jnp.arange(GRID_W, dtype=F32), rows_n)
    inv_freq = ROPE_THETA ** (-jnp.arange(0, AXIS_DIM, 2, dtype=F32) / AXIS_DIM)
    ang_r = (rows[:, None] * inv_freq).T
    ang_c = (cols[:, None] * inv_freq).T
    cos_t = jnp.concatenate([jnp.cos(ang_r)] * 2 + [jnp.cos(ang_c)] * 2, axis=0)
    sin_t = jnp.concatenate([-jnp.sin(ang_r), jnp.sin(ang_r), -jnp.sin(ang_c), jnp.sin(ang_c)],
                            axis=0)
    return cos_t, sin_t


def _rope_partner(g):
    h = AXIS_DIM // 2
    return jnp.concatenate([g[h:2 * h], g[0:h], g[3 * h:4 * h], g[2 * h:3 * h]])


def _t5_bucket(rel):
    nb = N_BUCKETS // 2
    max_exact = nb // 2
    base = (rel > 0).astype(jnp.int32) * nb
    n = jnp.abs(rel)
    nf = jnp.maximum(n, max_exact).astype(F32)
    large = max_exact + (jnp.log(nf / max_exact) / math.log(MAX_DISTANCE / max_exact)
                         * (nb - max_exact)).astype(jnp.int32)
    large = jnp.minimum(large, nb - 1)
    return base + jnp.where(n < max_exact, n, large)


def _window_bias_t(rel_bias_table):
    C = 3 * WINDOW
    n_diag = C + WINDOW - 1
    rel = jnp.arange(n_diag + 1) - (2 * WINDOW - 1)
    bucket = _t5_bucket(rel)[:, None]
    table = rel_bias_table.astype(F32)
    f = sum(jnp.where(bucket == b, table[b], 0.0) for b in range(N_BUCKETS)) * LOG2E
    f = jnp.where((jnp.abs(rel) <= WINDOW)[:, None], f, MASKED)
    skew = jnp.tile(f, (WINDOW, 1))[:WINDOW * n_diag].reshape(WINDOW, n_diag, N_HEADS)
    bias = skew[:, WINDOW - 1:WINDOW - 1 + C]
    bias = bias.transpose(2, 1, 0).reshape(N_KV_HEADS, GQA_GROUP, C, WINDOW)
    return bias.transpose(0, 2, 1, 3).reshape(N_KV_HEADS, C, GQA_GROUP * WINDOW)


def _trunk(x, a_wt, a_q_gain, a_k_gain, a_wo, b_wt, b_sink, b_wo, rel_bias_table,
           ln1_g, ln1_b, wg, wu, wd, ln2_g, ln2_b):
    B, S, _ = x.shape
    T = B * S
    tm_qkv = min(2048, S)
    tm_tok = min(1024, T)
    bq = min(1024, S)
    bkc = min(1024, S)
    qb = min(16, S // WINDOW)
    cos_t, sin_t = _rope_tables_t(S)
    for i in range(DEPTH):
        j = i // 2
        if i % 2 == 0:
            gq = a_q_gain[j].astype(F32) * Q_PRESCALE
            gk = a_k_gain[j].astype(F32)
            tables = (gq[:, None] * cos_t, _rope_partner(gq)[:, None] * sin_t,
                      gk[:, None] * cos_t, _rope_partner(gk)[:, None] * sin_t)
            qt, k, vt = _qkv_project(x, a_wt[j], tables, bq=bq, tm=tm_qkv)
            bound = HEAD_DIM * jnp.max(jnp.abs(gq)) * jnp.max(jnp.abs(gk)) * BF16_ROUNDING_SLACK
            n_split = GQA_GROUP * bq // 512
            attn = functools.partial(_global_attention, bq=bq, bkc=512, n_split=n_split)
            attn_bounded = functools.partial(_global_attention_bounded, bq=bq, bkc=bkc,
                                             n_split=n_split, unroll=2)
            o = lax.cond(bound <= SAFE_LOG2_SPAN, attn_bounded, attn, qt, k, vt)
            wo = a_wo[j]
        else:
            qt, k, vt, norms = _qkv_project(x, b_wt[j], (), bq=WINDOW, tm=tm_qkv)
            bias_t = _window_bias_t(rel_bias_table)
            sink_t = jnp.repeat(b_sink[j].astype(F32) * LOG2E, WINDOW).reshape(
                N_KV_HEADS, 1, GQA_GROUP * WINDOW)
            qk_bound = jnp.sqrt(jnp.max(norms[:, :, 0]) * jnp.max(norms[:, :, 1])) * BF16_ROUNDING_SLACK
            bias_bound = jnp.max(jnp.where(bias_t > MASKED, jnp.abs(bias_t), 0.0))
            bounded = ((qk_bound + bias_bound <= SAFE_LOG2_SPAN)
                       & (jnp.max(jnp.abs(sink_t)) <= SAFE_LOG2_SPAN))
            window = functools.partial(_window_attention, qb=qb)
            o = lax.cond(bounded, functools.partial(window, bounded=True),
                         functools.partial(window, bounded=False), qt, k, vt, bias_t, sink_t)
            wo = b_wo[j]
        x2 = _post_attention(x.reshape(T, D_MODEL), o.reshape(T, Q_DIM), wo,
                             ln1_g[i][None], ln1_b[i][None], wg, wu, wd,
                             ln2_g[i][None], ln2_b[i][None], layer=i, tm=tm_tok)
        x = x2.reshape(B, S, D_MODEL)
    return x


def _prepare_weights(a_w_qkv, a_q_gain, a_k_gain, a_w_o, b_w_qkv, b_sink, b_w_o, rel_bias_table,
                     ln1_g, ln1_b, w_gate, w_up, w_down, ln2_g, ln2_b):
    return (jnp.swapaxes(a_w_qkv, 1, 2).astype(BF16), a_q_gain, a_k_gain, a_w_o.astype(BF16),
            jnp.swapaxes(b_w_qkv, 1, 2).astype(BF16), b_sink, b_w_o.astype(BF16),
            rel_bias_table, ln1_g, ln1_b, w_gate.astype(BF16), w_up.astype(BF16),
            w_down.astype(BF16), ln2_g, ln2_b)


def kernel(x_prompt, x_sample, a_w_qkv, a_q_gain, a_k_gain, a_w_o, b_w_qkv, b_sink, b_w_o,
           rel_bias_table, ln1_g, ln1_b, w_gate, w_up, w_down, ln2_g, ln2_b):
    weights = _prepare_weights(a_w_qkv, a_q_gain, a_k_gain, a_w_o, b_w_qkv, b_sink, b_w_o,
                               rel_bias_table, ln1_g, ln1_b, w_gate, w_up, w_down, ln2_g, ln2_b)
    return (_trunk(x_prompt, *weights), _trunk(x_sample, *weights))
```

```python
import functools
import math

import jax
import jax.numpy as jnp
from jax import lax
from jax.experimental import pallas as pl
from jax.experimental.pallas import tpu as pltpu

D_MODEL = 1024
DEPTH = 2
N_HEADS = 8
N_KV_HEADS = 2
HEAD_DIM = 128
GQA_GROUP = N_HEADS // N_KV_HEADS
Q_DIM = N_HEADS * HEAD_DIM
KV_DIM = N_KV_HEADS * HEAD_DIM
QKV_DIM = Q_DIM + 2 * KV_DIM
D_FF = 2816
GRID_W = 64
AXIS_DIM = HEAD_DIM // 2
ROPE_THETA = 10000.0
WINDOW = 128
N_BUCKETS = 32
MAX_DISTANCE = 128
LN_EPS = 1e-5
RMS_EPS = 1e-6
DEEPNORM_ALPHA = (2.0 * DEPTH) ** 0.25
LOG2E = math.log2(math.e)
Q_PRESCALE = HEAD_DIM ** -0.5 * LOG2E
MASKED = -1e30
SAFE_LOG2_SPAN = 48.0
BF16_ROUNDING_SLACK = 1.02

ATTN_COLUMN_GROUP = 512

V7X_VMEM_LIMIT_BYTES = 56 * 1024 * 1024
V7X_F32_TILE = (8, 128)

BF16 = jnp.bfloat16
F32 = jnp.float32


def _params(sem):
    return pltpu.CompilerParams(dimension_semantics=sem, vmem_limit_bytes=V7X_VMEM_LIMIT_BYTES)


def _const_spec(shape):
    nd = len(shape)
    return pl.BlockSpec(shape, lambda *_: (0,) * nd, pipeline_mode=pl.Buffered(1))


HEADS_PER_DOT = 2


def _qkv_kernel(x_ref, wt_ref, *refs, norm_rope, bq):
    if norm_rope:
        cos_ref, sin_ref, gq_ref, gk_ref, qt_ref, k_ref, vt_ref = refs
    else:
        qt_ref, k_ref, vt_ref, norm_ref = refs
    tm = x_ref.shape[1]
    xb = x_ref[0].astype(BF16)
    max_sq = {"q": None, "k": None}

    def track_norm(name, slab):
        sq = jnp.sum(slab * slab, axis=0, keepdims=True)
        max_sq[name] = sq if max_sq[name] is None else jnp.maximum(max_sq[name], sq)

    def norm_rope_slab(y, gain_ref):
        r = lax.rsqrt(jnp.mean(y * y, axis=0, keepdims=True) + RMS_EPS)
        z = y * gain_ref[...]
        h = AXIS_DIM // 2
        partner = jnp.concatenate([z[h:2 * h], z[0:h], z[3 * h:4 * h], z[2 * h:3 * h]], axis=0)
        return (z * cos_ref[...] + partner * sin_ref[...]) * r

    rows = HEADS_PER_DOT * HEAD_DIM
    for grp in range(QKV_DIM // rows):
        yt = lax.dot_general(wt_ref[grp * rows:(grp + 1) * rows, :], xb, (((1,), (1,)), ((), ())),
                             preferred_element_type=F32)
        for sub in range(HEADS_PER_DOT):
            slab = yt[sub * HEAD_DIM:(sub + 1) * HEAD_DIM]
            head = grp * HEADS_PER_DOT + sub
            if head < N_HEADS:
                if norm_rope:
                    slab = norm_rope_slab(slab, gq_ref)
                else:
                    slab = slab * Q_PRESCALE
                    track_norm("q", slab)
                slab = slab.astype(BF16)
                kvh, g = divmod(head, GQA_GROUP)
                for j in range(tm // bq):
                    col = (j * GQA_GROUP + g) * bq
                    qt_ref[0, kvh, :, col:col + bq] = slab[:, j * bq:(j + 1) * bq]
            elif head < N_HEADS + N_KV_HEADS:
                kvh = head - N_HEADS
                if norm_rope:
                    slab = norm_rope_slab(slab, gk_ref)
                else:
                    track_norm("k", slab)
                k_ref[0, kvh] = slab.T.astype(BF16)
            else:
                kvh = head - N_HEADS - N_KV_HEADS
                vt_ref[0, kvh * HEAD_DIM:(kvh + 1) * HEAD_DIM, :] = slab.astype(BF16)
    if not norm_rope:
        row = lax.broadcasted_iota(jnp.int32, norm_ref.shape[2:], 0)
        norm_ref[0, 0] = jnp.where(row == 0, jnp.max(max_sq["q"], axis=1, keepdims=True),
                                   jnp.max(max_sq["k"], axis=1, keepdims=True))


def _qkv_project(x, wt, rope, *, bq, tm):
    B, S, _ = x.shape
    kern = functools.partial(_qkv_kernel, norm_rope=bool(rope), bq=bq)
    rope_specs = []
    if rope:
        table = pl.BlockSpec((HEAD_DIM, tm), lambda b, i: (0, i))
        rope_specs = [table, table, _const_spec((HEAD_DIM, 1)), _const_spec((HEAD_DIM, 1))]
    out_specs = [
        pl.BlockSpec((1, N_KV_HEADS, HEAD_DIM, GQA_GROUP * tm), lambda b, i: (b, 0, 0, i)),
        pl.BlockSpec((1, N_KV_HEADS, tm, HEAD_DIM), lambda b, i: (b, 0, i, 0)),
        pl.BlockSpec((1, KV_DIM, tm), lambda b, i: (b, 0, i)),
    ]
    out_shape = [
        jax.ShapeDtypeStruct((B, N_KV_HEADS, HEAD_DIM, GQA_GROUP * S), BF16),
        jax.ShapeDtypeStruct((B, N_KV_HEADS, S, HEAD_DIM), BF16),
        jax.ShapeDtypeStruct((B, KV_DIM, S), BF16),
    ]
    if not rope:
        out_specs.append(pl.BlockSpec((1, 1) + V7X_F32_TILE, lambda b, i: (b, i, 0, 0)))
        out_shape.append(jax.ShapeDtypeStruct((B, S // tm) + V7X_F32_TILE, F32))
    return pl.pallas_call(
        kern,
        grid=(B, S // tm),
        in_specs=[
            pl.BlockSpec((1, tm, D_MODEL), lambda b, i: (b, i, 0)),
            _const_spec((QKV_DIM, D_MODEL)),
        ] + rope_specs,
        out_specs=out_specs,
        out_shape=out_shape,
        compiler_params=_params(("parallel", "parallel")),
        name="qkv_project",
    )(x, wt, *rope)


def _store_heads(o_ref, out_t, bq):
    for g in range(GQA_GROUP):
        o_ref[0, :, g * HEAD_DIM:(g + 1) * HEAD_DIM] = out_t[:, g * bq:(g + 1) * bq].T.astype(BF16)


def _global_attn_kernel(qt_ref, k_ref, vt_ref, o_ref, acc_ref, *, bq, bkc, n_split):
    S = k_ref.shape[2]
    nq = GQA_GROUP * bq
    w = nq // n_split
    acc_ref[...] = jnp.zeros_like(acc_ref)

    def body(c, carry):
        ms, ls = carry
        start = pl.multiple_of(c * bkc, bkc)
        k = k_ref[0, 0, pl.ds(start, bkc), :]
        vt = vt_ref[0, :, pl.ds(start, bkc)]
        new_ms, new_ls = [], []
        for h in range(n_split):
            q = qt_ref[0, 0, :, h * w:(h + 1) * w]
            s = jnp.dot(k, q, preferred_element_type=F32)
            m_new = jnp.maximum(ms[h], jnp.max(s, axis=0, keepdims=True))
            alpha = jnp.exp2(ms[h] - m_new)
            p = jnp.exp2(s - m_new)
            new_ls.append(alpha * ls[h] + jnp.sum(p, axis=0, keepdims=True))
            new_ms.append(m_new)
            pv = jnp.dot(vt, p.astype(BF16), preferred_element_type=F32)
            acc_ref[:, h * w:(h + 1) * w] = alpha * acc_ref[:, h * w:(h + 1) * w] + pv
        return tuple(new_ms), tuple(new_ls)

    init = (tuple(jnp.full((1, w), MASKED, F32) for _ in range(n_split)),
            tuple(jnp.zeros((1, w), F32) for _ in range(n_split)))
    _, ls = lax.fori_loop(0, S // bkc, body, init)
    l = jnp.concatenate(ls, axis=1)
    _store_heads(o_ref, acc_ref[...] * (1.0 / l), bq)


def _global_attn_bounded_kernel(qt_ref, k_ref, vt_ref, o_ref, acc_ref, *, bq, bkc, n_split,
                                unroll):
    S = k_ref.shape[2]
    nq = GQA_GROUP * bq
    w = nq // n_split
    acc_ref[...] = jnp.zeros_like(acc_ref)

    def body(c, ls):
        start = pl.multiple_of(c * bkc, bkc)
        k = k_ref[0, 0, pl.ds(start, bkc), :]
        vt = vt_ref[0, :, pl.ds(start, bkc)]
        ss = [jnp.dot(k, qt_ref[0, 0, :, h * w:(h + 1) * w], preferred_element_type=F32)
              for h in range(n_split)]
        new_ls = []
        for h in range(n_split):
            cols = slice(h * w, (h + 1) * w)
            p = jnp.exp2(ss[h])
            new_ls.append(ls[h] + jnp.sum(p.reshape(bkc // 8, 8, w), axis=0))
            acc_ref[:, cols] += jnp.dot(vt, p.astype(BF16), preferred_element_type=F32)
        return tuple(new_ls)

    ls = lax.fori_loop(0, S // bkc, body, tuple(jnp.zeros((8, w), F32) for _ in range(n_split)),
                       unroll=unroll)
    l = jnp.concatenate([jnp.sum(x, axis=0, keepdims=True) for x in ls], axis=1)
    _store_heads(o_ref, acc_ref[...] * (1.0 / l), bq)


def _global_attention_call(kern, name, qt, k, vt, *, bq):
    B, _, S, _ = k.shape
    nq = GQA_GROUP * bq
    return pl.pallas_call(
        kern,
        grid=(B, N_KV_HEADS, S // bq),
        in_specs=[
            pl.BlockSpec((1, 1, HEAD_DIM, nq), lambda b, h, i: (b, h, 0, i)),
            pl.BlockSpec((1, 1, S, HEAD_DIM), lambda b, h, i: (b, h, 0, 0)),
            pl.BlockSpec((1, HEAD_DIM, S), lambda b, h, i: (b, h, 0)),
        ],
        out_specs=pl.BlockSpec((1, bq, GQA_GROUP * HEAD_DIM), lambda b, h, i: (b, i, h)),
        out_shape=jax.ShapeDtypeStruct((B, S, Q_DIM), BF16),
        scratch_shapes=[pltpu.VMEM((HEAD_DIM, nq), F32)],
        compiler_params=_params(("parallel", "parallel", "arbitrary")),
        name=name,
    )(qt, k, vt)


def _global_attention_bounded(qt, k, vt, *, bq, bkc, n_split, unroll):
    kern = functools.partial(_global_attn_bounded_kernel, bq=bq, bkc=bkc, n_split=n_split,
                             unroll=unroll)
    return _global_attention_call(kern, "global_attention_bounded", qt, k, vt, bq=bq)


def _global_attention(qt, k, vt, *, bq, bkc, n_split):
    kern = functools.partial(_global_attn_kernel, bq=bq, bkc=bkc, n_split=n_split)
    return _global_attention_call(kern, "global_attention", qt, k, vt, bq=bq)


def _window_attn_kernel(qt_ref, k_ref, vt_ref, bias_ref, sink_ref, o_ref, s_ref, *, qb, bounded):
    step = pl.program_id(2)
    last_step = pl.num_programs(2) - 1
    nb = k_ref.shape[2] // WINDOW
    nq = GQA_GROUP * WINDOW
    sink = sink_ref[0]

    def block_start(n):
        return pl.multiple_of(n * WINDOW, WINDOW)

    starts = [[block_start(jnp.maximum(step * qb + qi - 1, 0)), block_start(step * qb + qi),
               block_start(jnp.minimum(step * qb + qi + 1, nb - 1))] for qi in range(qb)]
    for qi in range(qb):
        kw = jnp.concatenate([k_ref[0, 0, pl.ds(st, WINDOW), :] for st in starts[qi]], axis=0)
        s = jnp.dot(kw, qt_ref[0, 0, :, qi * nq:(qi + 1) * nq],
                    preferred_element_type=F32) + bias_ref[0]
        if qi == 0:
            s = jnp.concatenate(
                [jnp.where(step == 0, MASKED, s[:WINDOW]), s[WINDOW:]], axis=0)
        if qi == qb - 1:
            s = jnp.concatenate(
                [s[:2 * WINDOW], jnp.where(step == last_step, MASKED, s[2 * WINDOW:])], axis=0)
        s_ref[qi] = s
    for qi in range(qb):
        vw = jnp.concatenate([vt_ref[0, :, pl.ds(st, WINDOW)] for st in starts[qi]], axis=1)
        if bounded:
            p = jnp.exp2(s_ref[qi])
            l = jnp.sum(p, axis=0, keepdims=True) + jnp.exp2(sink)
        else:
            m = jnp.maximum(jnp.max(s_ref[qi], axis=0, keepdims=True), sink)
            p = jnp.exp2(s_ref[qi] - m)
            l = jnp.sum(p, axis=0, keepdims=True) + jnp.exp2(sink - m)
        out_t = jnp.dot(vw, p.astype(BF16), preferred_element_type=F32) * (1.0 / l)
        for g in range(GQA_GROUP):
            o_ref[0, qi * WINDOW:(qi + 1) * WINDOW, g * HEAD_DIM:(g + 1) * HEAD_DIM] = (
                out_t[:, g * WINDOW:(g + 1) * WINDOW].T.astype(BF16))


def _window_attention(qt, k, vt, bias_t, sink_t, *, qb, bounded):
    B, _, S, _ = k.shape
    nq = GQA_GROUP * WINDOW
    return pl.pallas_call(
        functools.partial(_window_attn_kernel, qb=qb, bounded=bounded),
        grid=(B, N_KV_HEADS, S // (qb * WINDOW)),
        in_specs=[
            pl.BlockSpec((1, 1, HEAD_DIM, qb * nq), lambda b, h, n: (b, h, 0, n)),
            pl.BlockSpec((1, 1, S, HEAD_DIM), lambda b, h, n: (b, h, 0, 0)),
            pl.BlockSpec((1, HEAD_DIM, S), lambda b, h, n: (b, h, 0)),
            pl.BlockSpec((1, 3 * WINDOW, nq), lambda b, h, n: (h, 0, 0)),
            pl.BlockSpec((1, 1, nq), lambda b, h, n: (h, 0, 0)),
        ],
        out_specs=pl.BlockSpec((1, qb * WINDOW, GQA_GROUP * HEAD_DIM), lambda b, h, n: (b, n, h)),
        out_shape=jax.ShapeDtypeStruct((B, S, Q_DIM), BF16),
        scratch_shapes=[pltpu.VMEM((qb, 3 * WINDOW, nq), F32)],
        compiler_params=_params(("parallel", "parallel", "parallel")),
        name="window_attention",
    )(qt, k, vt, bias_t, sink_t)


def _layer_norm(y, g, b):
    mu = jnp.mean(y, axis=-1, keepdims=True)
    d = y - mu
    var = jnp.mean(d * d, axis=-1, keepdims=True)
    return d * lax.rsqrt(var + LN_EPS) * g + b


ROW_GROUPS = 4


def _post_attention_kernel(x_ref, o_ref, wo_ref, g1_ref, b1_ref, wg_ref, wu_ref, wd_ref, g2_ref,
                           b2_ref, y_ref):
    tm = x_ref.shape[0]
    rows = [slice(r * tm // ROW_GROUPS, (r + 1) * tm // ROW_GROUPS) for r in range(ROW_GROUPS)]
    hs = [jnp.dot(o_ref[r, :], wo_ref[...], preferred_element_type=F32) for r in rows]
    x1s = [_layer_norm(DEEPNORM_ALPHA * x_ref[r, :] + h, g1_ref[...], b1_ref[...])
           for r, h in zip(rows, hs)]
    mids = []
    for x1 in x1s:
        xb = x1.astype(BF16)
        gate = jnp.dot(xb, wg_ref[...], preferred_element_type=F32)
        up = jnp.dot(xb, wu_ref[...], preferred_element_type=F32)
        mids.append((gate * jax.nn.sigmoid(gate) * up).astype(BF16))
    hs = [jnp.dot(mid, wd_ref[...], preferred_element_type=F32) for mid in mids]
    for r, x1, h in zip(rows, x1s, hs):
        y_ref[r, :] = _layer_norm(DEEPNORM_ALPHA * x1 + h, g2_ref[...], b2_ref[...])


def _post_attention(x, o, wo, g1, b1, wg, wu, wd, g2, b2, *, layer, tm):
    T = x.shape[0]
    vec = _const_spec((1, D_MODEL))

    def stacked(shape):
        return pl.BlockSpec((None,) + shape, lambda t: (layer, 0, 0), pipeline_mode=pl.Buffered(1))

    return pl.pallas_call(
        _post_attention_kernel,
        grid=(T // tm,),
        in_specs=[
            pl.BlockSpec((tm, D_MODEL), lambda i: (i, 0)),
            pl.BlockSpec((tm, Q_DIM), lambda i: (i, 0)),
            _const_spec((Q_DIM, D_MODEL)), vec, vec,
            stacked((D_MODEL, D_FF)),
            stacked((D_MODEL, D_FF)),
            stacked((D_FF, D_MODEL)), vec, vec,
        ],
        out_specs=pl.BlockSpec((tm, D_MODEL), lambda i: (i, 0)),
        out_shape=jax.ShapeDtypeStruct((T, D_MODEL), F32),
        compiler_params=_params(("parallel",)),
        name="post_attention",
    )(x, o, wo, g1, b1, wg, wu, wd, g2, b2)


def _rope_tables_t(seq_len):
    rows_n = seq_len // GRID_W
    rows = jnp.repeat(jnp.arange(rows_n, dtype=F32), GRID_W)
    cols = jnp.tile(jnp.arange(GRID_W, dtype=F32), rows_n)
    inv_freq = ROPE_THETA ** (-jnp.arange(0, AXIS_DIM, 2, dtype=F32) / AXIS_DIM)
    ang_r = (rows[:, None] * inv_freq).T
    ang_c = (cols[:, None] * inv_freq).T
    cos_t = jnp.concatenate([jnp.cos(ang_r)] * 2 + [jnp.cos(ang_c)] * 2, axis=0)
    sin_t = jnp.concatenate([-jnp.sin(ang_r), jnp.sin(ang_r), -jnp.sin(ang_c), jnp.sin(ang_c)],
                            axis=0)
    return cos_t, sin_t


def _t5_bucket(rel):
    nb = N_BUCKETS // 2
    max_exact = nb // 2
    base = (rel > 0).astype(jnp.int32) * nb
    n = jnp.abs(rel)
    nf = jnp.maximum(n, max_exact).astype(F32)
    large = max_exact + (jnp.log(nf / max_exact) / math.log(MAX_DISTANCE / max_exact)
                         * (nb - max_exact)).astype(jnp.int32)
    large = jnp.minimum(large, nb - 1)
    return base + jnp.where(n < max_exact, n, large)


def _window_bias_t(rel_bias_table):
    C = 3 * WINDOW
    n_diag = C + WINDOW - 1
    rel = jnp.arange(n_diag + 1) - (2 * WINDOW - 1)
    bucket = _t5_bucket(rel)[:, None]
    table = rel_bias_table.astype(F32)
    f = sum(jnp.where(bucket == b, table[b], 0.0) for b in range(N_BUCKETS)) * LOG2E
    f = jnp.where((jnp.abs(rel) <= WINDOW)[:, None], f, MASKED)
    skew = jnp.tile(f, (WINDOW, 1))[:WINDOW * n_diag].reshape(WINDOW, n_diag, N_HEADS)
    bias = skew[:, WINDOW - 1:WINDOW - 1 + C]
    bias = bias.transpose(2, 1, 0).reshape(N_KV_HEADS, GQA_GROUP, C, WINDOW)
    return bias.transpose(0, 2, 1, 3).reshape(N_KV_HEADS, C, GQA_GROUP * WINDOW)


def _trunk(x, a_wt, a_q_gain, a_k_gain, a_wo, b_wt, b_sink, b_wo, rel_bias_table,
           ln1_g, ln1_b, wg, wu, wd, ln2_g, ln2_b):
    B, S, _ = x.shape
    T = B * S
    tm_qkv = min(2048, S)
    tm_tok = min(1024, T)
    bq = min(1024, S)
    bkc = min(1024, S)
    qb = min(16, S // WINDOW)
    cos_t, sin_t = _rope_tables_t(S)
    for i in range(DEPTH):
        j = i // 2
        if i % 2 == 0:
            gq = a_q_gain[j].astype(F32) * Q_PRESCALE
            gk = a_k_gain[j].astype(F32)
            qt, k, vt = _qkv_project(x, a_wt[j], (cos_t, sin_t, gq[:, None], gk[:, None]),
                                     bq=bq, tm=tm_qkv)
            bound = HEAD_DIM * jnp.max(jnp.abs(gq)) * jnp.max(jnp.abs(gk)) * BF16_ROUNDING_SLACK
            n_split = GQA_GROUP * bq // ATTN_COLUMN_GROUP
            attn = functools.partial(_global_attention, bq=bq, bkc=512, n_split=n_split)
            attn_bounded = functools.partial(_global_attention_bounded, bq=bq, bkc=bkc,
                                             n_split=n_split, unroll=2)
            o = lax.cond(bound <= SAFE_LOG2_SPAN, attn_bounded, attn, qt, k, vt)
            wo = a_wo[j]
        else:
            qt, k, vt, norms = _qkv_project(x, b_wt[j], (), bq=WINDOW, tm=tm_qkv)
            bias_t = _window_bias_t(rel_bias_table)
            sink_t = jnp.repeat(b_sink[j].astype(F32) * LOG2E, WINDOW).reshape(
                N_KV_HEADS, 1, GQA_GROUP * WINDOW)
            qk_bound = jnp.sqrt(jnp.max(norms[:, :, 0]) * jnp.max(norms[:, :, 1])) * BF16_ROUNDING_SLACK
            bias_bound = jnp.max(jnp.where(bias_t > MASKED, jnp.abs(bias_t), 0.0))
            bounded = ((qk_bound + bias_bound <= SAFE_LOG2_SPAN)
                       & (jnp.max(jnp.abs(sink_t)) <= SAFE_LOG2_SPAN))
            window = functools.partial(_window_attention, qb=qb)
            o = lax.cond(bounded, functools.partial(window, bounded=True),
                         functools.partial(window, bounded=False), qt, k, vt, bias_t, sink_t)
            wo = b_wo[j]
        x2 = _post_attention(x.reshape(T, D_MODEL), o.reshape(T, Q_DIM), wo,
                             ln1_g[i][None], ln1_b[i][None], wg, wu, wd,
                             ln2_g[i][None], ln2_b[i][None], layer=i, tm=tm_tok)
        x = x2.reshape(B, S, D_MODEL)
    return x


def _prepare_weights(a_w_qkv, a_q_gain, a_k_gain, a_w_o, b_w_qkv, b_sink, b_w_o, rel_bias_table,
                     ln1_g, ln1_b, w_gate, w_up, w_down, ln2_g, ln2_b):
    return (jnp.swapaxes(a_w_qkv, 1, 2).astype(BF16), a_q_gain, a_k_gain, a_w_o.astype(BF16),
            jnp.swapaxes(b_w_qkv, 1, 2).astype(BF16), b_sink, b_w_o.astype(BF16),
            rel_bias_table, ln1_g, ln1_b, w_gate.astype(BF16), w_up.astype(BF16),
            w_down.astype(BF16), ln2_g, ln2_b)


def kernel(x_prompt, x_sample, a_w_qkv, a_q_gain, a_k_gain, a_w_o, b_w_qkv, b_sink, b_w_o,
           rel_bias_table, ln1_g, ln1_b, w_gate, w_up, w_down, ln2_g, ln2_b):
    weights = _prepare_weights(a_w_qkv, a_q_gain, a_k_gain, a_w_o, b_w_qkv, b_sink, b_w_o,
                               rel_bias_table, ln1_g, ln1_b, w_gate, w_up, w_down, ln2_g, ln2_b)
    return (_trunk(x_prompt, *weights), _trunk(x_sample, *weights))
```

```python
import functools
import math

import jax
import jax.numpy as jnp
from jax import lax
from jax.experimental import pallas as pl
from jax.experimental.pallas import tpu as pltpu

D_MODEL = 1024
DEPTH = 2
N_HEADS = 8
N_KV_HEADS = 2
HEAD_DIM = 128
GQA_GROUP = N_HEADS // N_KV_HEADS
Q_DIM = N_HEADS * HEAD_DIM
KV_DIM = N_KV_HEADS * HEAD_DIM
QKV_DIM = Q_DIM + 2 * KV_DIM
D_FF = 2816
GRID_W = 64
AXIS_DIM = HEAD_DIM // 2
ROPE_THETA = 10000.0
WINDOW = 128
N_BUCKETS = 32
MAX_DISTANCE = 128
LN_EPS = 1e-5
RMS_EPS = 1e-6
DEEPNORM_ALPHA = (2.0 * DEPTH) ** 0.25
LOG2E = math.log2(math.e)
Q_PRESCALE = HEAD_DIM ** -0.5 * LOG2E
MASKED = -1e30
SAFE_LOG2_SPAN = 48.0
BF16_ROUNDING_SLACK = 1.02

ATTN_COLUMN_GROUP = 512

V7X_VMEM_LIMIT_BYTES = 56 * 1024 * 1024
V7X_F32_TILE = (8, 128)

BF16 = jnp.bfloat16
F32 = jnp.float32


def _params(sem):
    return pltpu.CompilerParams(dimension_semantics=sem, vmem_limit_bytes=V7X_VMEM_LIMIT_BYTES)


def _const_spec(shape):
    nd = len(shape)
    return pl.BlockSpec(shape, lambda *_: (0,) * nd, pipeline_mode=pl.Buffered(1))


HEADS_PER_DOT = 2


def _qkv_kernel(x_ref, wt_ref, *refs, norm_rope, bq):
    if norm_rope:
        cos_ref, sin_ref, gq_ref, gk_ref, qt_ref, k_ref, vt_ref = refs
    else:
        qt_ref, k_ref, vt_ref, norm_ref = refs
    tm = x_ref.shape[1]
    xb = x_ref[0].astype(BF16)
    max_sq = {"q": None, "k": None}

    def track_norm(name, slab):
        sq = jnp.sum(slab * slab, axis=0, keepdims=True)
        max_sq[name] = sq if max_sq[name] is None else jnp.maximum(max_sq[name], sq)

    def norm_rope_slab(y, gain_ref):
        r = lax.rsqrt(jnp.mean(y * y, axis=0, keepdims=True) + RMS_EPS)
        z = y * gain_ref[...]
        h = AXIS_DIM // 2
        partner = jnp.concatenate([z[h:2 * h], z[0:h], z[3 * h:4 * h], z[2 * h:3 * h]], axis=0)
        return (z * cos_ref[...] + partner * sin_ref[...]) * r

    rows = HEADS_PER_DOT * HEAD_DIM
    for grp in range(QKV_DIM // rows):
        yt = lax.dot_general(wt_ref[grp * rows:(grp + 1) * rows, :], xb, (((1,), (1,)), ((), ())),
                             preferred_element_type=F32)
        for sub in range(HEADS_PER_DOT):
            slab = yt[sub * HEAD_DIM:(sub + 1) * HEAD_DIM]
            head = grp * HEADS_PER_DOT + sub
            if head < N_HEADS:
                if norm_rope:
                    slab = norm_rope_slab(slab, gq_ref)
                else:
                    slab = slab * Q_PRESCALE
                    track_norm("q", slab)
                slab = slab.astype(BF16)
                kvh, g = divmod(head, GQA_GROUP)
                for j in range(tm // bq):
                    col = (j * GQA_GROUP + g) * bq
                    qt_ref[0, kvh, :, col:col + bq] = slab[:, j * bq:(j + 1) * bq]
            elif head < N_HEADS + N_KV_HEADS:
                kvh = head - N_HEADS
                if norm_rope:
                    slab = norm_rope_slab(slab, gk_ref)
                else:
                    track_norm("k", slab)
                k_ref[0, kvh] = slab.T.astype(BF16)
            else:
                kvh = head - N_HEADS - N_KV_HEADS
                vt_ref[0, kvh * HEAD_DIM:(kvh + 1) * HEAD_DIM, :] = slab.astype(BF16)
    if not norm_rope:
        row = lax.broadcasted_iota(jnp.int32, norm_ref.shape[2:], 0)
        norm_ref[0, 0] = jnp.where(row == 0, jnp.max(max_sq["q"], axis=1, keepdims=True),
                                   jnp.max(max_sq["k"], axis=1, keepdims=True))


def _qkv_project(x, wt, rope, *, bq, tm):
    B, S, _ = x.shape
    kern = functools.partial(_qkv_kernel, norm_rope=bool(rope), bq=bq)
    rope_specs = []
    if rope:
        table = pl.BlockSpec((HEAD_DIM, tm), lambda b, i: (0, i))
        rope_specs = [table, table, _const_spec((HEAD_DIM, 1)), _const_spec((HEAD_DIM, 1))]
    out_specs = [
        pl.BlockSpec((1, N_KV_HEADS, HEAD_DIM, GQA_GROUP * tm), lambda b, i: (b, 0, 0, i)),
        pl.BlockSpec((1, N_KV_HEADS, tm, HEAD_DIM), lambda b, i: (b, 0, i, 0)),
        pl.BlockSpec((1, KV_DIM, tm), lambda b, i: (b, 0, i)),
    ]
    out_shape = [
        jax.ShapeDtypeStruct((B, N_KV_HEADS, HEAD_DIM, GQA_GROUP * S), BF16),
        jax.ShapeDtypeStruct((B, N_KV_HEADS, S, HEAD_DIM), BF16),
        jax.ShapeDtypeStruct((B, KV_DIM, S), BF16),
    ]
    if not rope:
        out_specs.append(pl.BlockSpec((1, 1) + V7X_F32_TILE, lambda b, i: (b, i, 0, 0)))
        out_shape.append(jax.ShapeDtypeStruct((B, S // tm) + V7X_F32_TILE, F32))
    return pl.pallas_call(
        kern,
        grid=(B, S // tm),
        in_specs=[
            pl.BlockSpec((1, tm, D_MODEL), lambda b, i: (b, i, 0)),
            _const_spec((QKV_DIM, D_MODEL)),
        ] + rope_specs,
        out_specs=out_specs,
        out_shape=out_shape,
        compiler_params=_params(("parallel", "parallel")),
        name="qkv_project",
    )(x, wt, *rope)


def _store_heads(o_ref, out_t, bq):
    for g in range(GQA_GROUP):
        o_ref[0, :, g * HEAD_DIM:(g + 1) * HEAD_DIM] = out_t[:, g * bq:(g + 1) * bq].T.astype(BF16)


def _global_attn_kernel(qt_ref, k_ref, vt_ref, o_ref, acc_ref, *, bq, bkc, n_split):
    S = k_ref.shape[2]
    nq = GQA_GROUP * bq
    w = nq // n_split
    acc_ref[...] = jnp.zeros_like(acc_ref)

    def body(c, carry):
        ms, ls = carry
        start = pl.multiple_of(c * bkc, bkc)
        k = k_ref[0, 0, pl.ds(start, bkc), :]
        vt = vt_ref[0, :, pl.ds(start, bkc)]
        new_ms, new_ls = [], []
        for h in range(n_split):
            q = qt_ref[0, 0, :, h * w:(h + 1) * w]
            s = jnp.dot(k, q, preferred_element_type=F32)
            m_new = jnp.maximum(ms[h], jnp.max(s, axis=0, keepdims=True))
            alpha = jnp.exp2(ms[h] - m_new)
            p = jnp.exp2(s - m_new)
            new_ls.append(alpha * ls[h] + jnp.sum(p, axis=0, keepdims=True))
            new_ms.append(m_new)
            pv = jnp.dot(vt, p.astype(BF16), preferred_element_type=F32)
            acc_ref[:, h * w:(h + 1) * w] = alpha * acc_ref[:, h * w:(h + 1) * w] + pv
        return tuple(new_ms), tuple(new_ls)

    init = (tuple(jnp.full((1, w), MASKED, F32) for _ in range(n_split)),
            tuple(jnp.zeros((1, w), F32) for _ in range(n_split)))
    _, ls = lax.fori_loop(0, S // bkc, body, init)
    l = jnp.concatenate(ls, axis=1)
    _store_heads(o_ref, acc_ref[...] * (1.0 / l), bq)


def _global_attn_bounded_kernel(qt_ref, k_ref, vt_ref, o_ref, acc_ref, *, bq, bkc, n_split,
                                unroll):
    S = k_ref.shape[2]
    nq = GQA_GROUP * bq
    w = nq // n_split
    sub = V7X_F32_TILE[0]
    acc_ref[...] = jnp.zeros_like(acc_ref)

    def body(c, ls):
        start = pl.multiple_of(c * bkc, bkc)
        k = k_ref[0, 0, pl.ds(start, bkc), :]
        vt = vt_ref[0, :, pl.ds(start, bkc)]
        ss = [jnp.dot(k, qt_ref[0, 0, :, h * w:(h + 1) * w], preferred_element_type=F32)
              for h in range(n_split)]
        new_ls = []
        for h in range(n_split):
            cols = slice(h * w, (h + 1) * w)
            p = jnp.exp2(ss[h])
            new_ls.append(ls[h] + jnp.sum(p.reshape(bkc // sub, sub, w), axis=0))
            acc_ref[:, cols] += jnp.dot(vt, p.astype(BF16), preferred_element_type=F32)
        return tuple(new_ls)

    ls = lax.fori_loop(0, S // bkc, body, tuple(jnp.zeros((sub, w), F32) for _ in range(n_split)),
                       unroll=unroll)
    l = jnp.concatenate([jnp.sum(x, axis=0, keepdims=True) for x in ls], axis=1)
    _store_heads(o_ref, acc_ref[...] * (1.0 / l), bq)


def _global_attention_call(kern, name, qt, k, vt, *, bq):
    B, _, S, _ = k.shape
    nq = GQA_GROUP * bq
    return pl.pallas_call(
        kern,
        grid=(B, N_KV_HEADS, S // bq),
        in_specs=[
            pl.BlockSpec((1, 1, HEAD_DIM, nq), lambda b, h, i: (b, h, 0, i)),
            pl.BlockSpec((1, 1, S, HEAD_DIM), lambda b, h, i: (b, h, 0, 0)),
            pl.BlockSpec((1, HEAD_DIM, S), lambda b, h, i: (b, h, 0)),
        ],
        out_specs=pl.BlockSpec((1, bq, GQA_GROUP * HEAD_DIM), lambda b, h, i: (b, i, h)),
        out_shape=jax.ShapeDtypeStruct((B, S, Q_DIM), BF16),
        scratch_shapes=[pltpu.VMEM((HEAD_DIM, nq), F32)],
        compiler_params=_params(("parallel", "parallel", "arbitrary")),
        name=name,
    )(qt, k, vt)


def _global_attention_bounded(qt, k, vt, *, bq, bkc, n_split, unroll):
    kern = functools.partial(_global_attn_bounded_kernel, bq=bq, bkc=bkc, n_split=n_split,
                             unroll=unroll)
    return _global_attention_call(kern, "global_attention_bounded", qt, k, vt, bq=bq)


def _global_attention(qt, k, vt, *, bq, bkc, n_split):
    kern = functools.partial(_global_attn_kernel, bq=bq, bkc=bkc, n_split=n_split)
    return _global_attention_call(kern, "global_attention", qt, k, vt, bq=bq)


def _window_attn_kernel(qt_ref, k_ref, vt_ref, bias_ref, sink_ref, o_ref, s_ref, *, qb, bounded):
    step = pl.program_id(2)
    last_step = pl.num_programs(2) - 1
    nb = k_ref.shape[2] // WINDOW
    nq = GQA_GROUP * WINDOW
    sink = sink_ref[0]

    def block_start(n):
        return pl.multiple_of(n * WINDOW, WINDOW)

    starts = [[block_start(jnp.maximum(step * qb + qi - 1, 0)), block_start(step * qb + qi),
               block_start(jnp.minimum(step * qb + qi + 1, nb - 1))] for qi in range(qb)]
    for qi in range(qb):
        kw = jnp.concatenate([k_ref[0, 0, pl.ds(st, WINDOW), :] for st in starts[qi]], axis=0)
        s = jnp.dot(kw, qt_ref[0, 0, :, qi * nq:(qi + 1) * nq],
                    preferred_element_type=F32) + bias_ref[0]
        if qi == 0:
            s = jnp.concatenate(
                [jnp.where(step == 0, MASKED, s[:WINDOW]), s[WINDOW:]], axis=0)
        if qi == qb - 1:
            s = jnp.concatenate(
                [s[:2 * WINDOW], jnp.where(step == last_step, MASKED, s[2 * WINDOW:])], axis=0)
        s_ref[qi] = s
    for qi in range(qb):
        vw = jnp.concatenate([vt_ref[0, :, pl.ds(st, WINDOW)] for st in starts[qi]], axis=1)
        if bounded:
            p = jnp.exp2(s_ref[qi])
            l = jnp.sum(p, axis=0, keepdims=True) + jnp.exp2(sink)
        else:
            m = jnp.maximum(jnp.max(s_ref[qi], axis=0, keepdims=True), sink)
            p = jnp.exp2(s_ref[qi] - m)
            l = jnp.sum(p, axis=0, keepdims=True) + jnp.exp2(sink - m)
        out_t = jnp.dot(vw, p.astype(BF16), preferred_element_type=F32) * (1.0 / l)
        for g in range(GQA_GROUP):
            o_ref[0, qi * WINDOW:(qi + 1) * WINDOW, g * HEAD_DIM:(g + 1) * HEAD_DIM] = (
                out_t[:, g * WINDOW:(g + 1) * WINDOW].T.astype(BF16))


def _window_attention(qt, k, vt, bias_t, sink_t, *, qb, bounded):
    B, _, S, _ = k.shape
    nq = GQA_GROUP * WINDOW
    return pl.pallas_call(
        functools.partial(_window_attn_kernel, qb=qb, bounded=bounded),
        grid=(B, N_KV_HEADS, S // (qb * WINDOW)),
        in_specs=[
            pl.BlockSpec((1, 1, HEAD_DIM, qb * nq), lambda b, h, n: (b, h, 0, n)),
            pl.BlockSpec((1, 1, S, HEAD_DIM), lambda b, h, n: (b, h, 0, 0)),
            pl.BlockSpec((1, HEAD_DIM, S), lambda b, h, n: (b, h, 0)),
            pl.BlockSpec((1, 3 * WINDOW, nq), lambda b, h, n: (h, 0, 0)),
            pl.BlockSpec((1, 1, nq), lambda b, h, n: (h, 0, 0)),
        ],
        out_specs=pl.BlockSpec((1, qb * WINDOW, GQA_GROUP * HEAD_DIM), lambda b, h, n: (b, n, h)),
        out_shape=jax.ShapeDtypeStruct((B, S, Q_DIM), BF16),
        scratch_shapes=[pltpu.VMEM((qb, 3 * WINDOW, nq), F32)],
        compiler_params=_params(("parallel", "parallel", "parallel")),
        name="window_attention",
    )(qt, k, vt, bias_t, sink_t)


def _layer_norm(y, g, b):
    mu = jnp.mean(y, axis=-1, keepdims=True)
    d = y - mu
    var = jnp.mean(d * d, axis=-1, keepdims=True)
    return d * lax.rsqrt(var + LN_EPS) * g + b


ROW_GROUPS = 4


def _post_attention_kernel(x_ref, o_ref, wo_ref, g1_ref, b1_ref, wg_ref, wu_ref, wd_ref, g2_ref,
                           b2_ref, y_ref):
    tm = x_ref.shape[0]
    rows = [slice(r * tm // ROW_GROUPS, (r + 1) * tm // ROW_GROUPS) for r in range(ROW_GROUPS)]
    hs = [jnp.dot(o_ref[r, :], wo_ref[...], preferred_element_type=F32) for r in rows]
    x1s = [_layer_norm(DEEPNORM_ALPHA * x_ref[r, :] + h, g1_ref[...], b1_ref[...])
           for r, h in zip(rows, hs)]
    mids = []
    for x1 in x1s:
        xb = x1.astype(BF16)
        gate = jnp.dot(xb, wg_ref[...], preferred_element_type=F32)
        up = jnp.dot(xb, wu_ref[...], preferred_element_type=F32)
        mids.append((gate * jax.nn.sigmoid(gate) * up).astype(BF16))
    hs = [jnp.dot(mid, wd_ref[...], preferred_element_type=F32) for mid in mids]
    for r, x1, h in zip(rows, x1s, hs):
        y_ref[r, :] = _layer_norm(DEEPNORM_ALPHA * x1 + h, g2_ref[...], b2_ref[...])


def _post_attention(x, o, wo, g1, b1, wg, wu, wd, g2, b2, *, layer, tm):
    T = x.shape[0]
    vec = _const_spec((1, D_MODEL))

    def stacked(shape):
        return pl.BlockSpec((None,) + shape, lambda t: (layer, 0, 0), pipeline_mode=pl.Buffered(1))

    return pl.pallas_call(
        _post_attention_kernel,
        grid=(T // tm,),
        in_specs=[
            pl.BlockSpec((tm, D_MODEL), lambda i: (i, 0)),
            pl.BlockSpec((tm, Q_DIM), lambda i: (i, 0)),
            _const_spec((Q_DIM, D_MODEL)), vec, vec,
            stacked((D_MODEL, D_FF)),
            stacked((D_MODEL, D_FF)),
            stacked((D_FF, D_MODEL)), vec, vec,
        ],
        out_specs=pl.BlockSpec((tm, D_MODEL), lambda i: (i, 0)),
        out_shape=jax.ShapeDtypeStruct((T, D_MODEL), F32),
        compiler_params=_params(("parallel",)),
        name="post_attention",
    )(x, o, wo, g1, b1, wg, wu, wd, g2, b2)


def _rope_tables_t(seq_len):
    rows_n = seq_len // GRID_W
    inv_freq = ROPE_THETA ** (-jnp.arange(0, AXIS_DIM, 2, dtype=F32) / AXIS_DIM)
    ang_r = (jnp.arange(rows_n, dtype=F32)[:, None] * inv_freq).T
    ang_c = (jnp.arange(GRID_W, dtype=F32)[:, None] * inv_freq).T
    over_rows = lambda a: jnp.repeat(a, GRID_W, axis=1)
    over_cols = lambda a: jnp.tile(a, (1, rows_n))
    cos_r, sin_r = over_rows(jnp.cos(ang_r)), over_rows(jnp.sin(ang_r))
    cos_c, sin_c = over_cols(jnp.cos(ang_c)), over_cols(jnp.sin(ang_c))
    cos_t = jnp.concatenate([cos_r, cos_r, cos_c, cos_c], axis=0)
    sin_t = jnp.concatenate([-sin_r, sin_r, -sin_c, sin_c], axis=0)
    return cos_t, sin_t


def _t5_bucket(rel):
    nb = N_BUCKETS // 2
    max_exact = nb // 2
    base = (rel > 0).astype(jnp.int32) * nb
    n = jnp.abs(rel)
    nf = jnp.maximum(n, max_exact).astype(F32)
    large = max_exact + (jnp.log(nf / max_exact) / math.log(MAX_DISTANCE / max_exact)
                         * (nb - max_exact)).astype(jnp.int32)
    large = jnp.minimum(large, nb - 1)
    return base + jnp.where(n < max_exact, n, large)


def _window_bias_t(rel_bias_table):
    C = 3 * WINDOW
    n_diag = C + WINDOW - 1
    rel = jnp.arange(n_diag + 1) - (2 * WINDOW - 1)
    bucket = _t5_bucket(rel)[:, None]
    table = rel_bias_table.astype(F32)
    f = sum(jnp.where(bucket == b, table[b], 0.0) for b in range(N_BUCKETS)) * LOG2E
    f = jnp.where((jnp.abs(rel) <= WINDOW)[:, None], f, MASKED)
    skew = jnp.tile(f, (WINDOW, 1))[:WINDOW * n_diag].reshape(WINDOW, n_diag, N_HEADS)
    bias = skew[:, WINDOW - 1:WINDOW - 1 + C]
    bias = bias.transpose(2, 1, 0).reshape(N_KV_HEADS, GQA_GROUP, C, WINDOW)
    return bias.transpose(0, 2, 1, 3).reshape(N_KV_HEADS, C, GQA_GROUP * WINDOW)


def _trunk(x, a_wt, a_q_gain, a_k_gain, a_wo, b_wt, b_sink, b_wo, rel_bias_table,
           ln1_g, ln1_b, wg, wu, wd, ln2_g, ln2_b):
    B, S, _ = x.shape
    T = B * S
    tm_qkv = min(2048, S)
    tm_tok = min(1024, T)
    bq = min(1024, S)
    bkc = min(1024, S)
    qb = min(16, S // WINDOW)
    cos_t, sin_t = _rope_tables_t(S)
    for i in range(DEPTH):
        j = i // 2
        if i % 2 == 0:
            gq = a_q_gain[j].astype(F32) * Q_PRESCALE
            gk = a_k_gain[j].astype(F32)
            qt, k, vt = _qkv_project(x, a_wt[j], (cos_t, sin_t, gq[:, None], gk[:, None]),
                                     bq=bq, tm=tm_qkv)
            bound = HEAD_DIM * jnp.max(jnp.abs(gq)) * jnp.max(jnp.abs(gk)) * BF16_ROUNDING_SLACK
            n_split = GQA_GROUP * bq // ATTN_COLUMN_GROUP
            attn = functools.partial(_global_attention, bq=bq, bkc=512, n_split=n_split)
            attn_bounded = functools.partial(_global_attention_bounded, bq=bq, bkc=bkc,
                                             n_split=n_split, unroll=2)
            o = lax.cond(bound <= SAFE_LOG2_SPAN, attn_bounded, attn, qt, k, vt)
            wo = a_wo[j]
        else:
            qt, k, vt, norms = _qkv_project(x, b_wt[j], (), bq=WINDOW, tm=tm_qkv)
            bias_t = _window_bias_t(rel_bias_table)
            sink_t = jnp.repeat(b_sink[j].astype(F32) * LOG2E, WINDOW).reshape(
                N_KV_HEADS, 1, GQA_GROUP * WINDOW)
            qk_bound = jnp.sqrt(jnp.max(norms[:, :, 0]) * jnp.max(norms[:, :, 1])) * BF16_ROUNDING_SLACK
            bias_bound = jnp.max(jnp.where(bias_t > MASKED, jnp.abs(bias_t), 0.0))
            bounded = ((qk_bound + bias_bound <= SAFE_LOG2_SPAN)
                       & (jnp.max(jnp.abs(sink_t)) <= SAFE_LOG2_SPAN))
            window = functools.partial(_window_attention, qb=qb)
            o = lax.cond(bounded, functools.partial(window, bounded=True),
                         functools.partial(window, bounded=False), qt, k, vt, bias_t, sink_t)
            wo = b_wo[j]
        x2 = _post_attention(x.reshape(T, D_MODEL), o.reshape(T, Q_DIM), wo,
                             ln1_g[i][None], ln1_b[i][None], wg, wu, wd,
                             ln2_g[i][None], ln2_b[i][None], layer=i, tm=tm_tok)
        x = x2.reshape(B, S, D_MODEL)
    return x


def _prepare_weights(a_w_qkv, a_q_gain, a_k_gain, a_w_o, b_w_qkv, b_sink, b_w_o, rel_bias_table,
                     ln1_g, ln1_b, w_gate, w_up, w_down, ln2_g, ln2_b):
    return (jnp.swapaxes(a_w_qkv.astype(BF16), 1, 2), a_q_gain, a_k_gain, a_w_o.astype(BF16),
            jnp.swapaxes(b_w_qkv.astype(BF16), 1, 2), b_sink, b_w_o.astype(BF16),
            rel_bias_table, ln1_g, ln1_b, w_gate.astype(BF16), w_up.astype(BF16),
            w_down.astype(BF16), ln2_g, ln2_b)


def kernel(x_prompt, x_sample, a_w_qkv, a_q_gain, a_k_gain, a_w_o, b_w_qkv, b_sink, b_w_o,
           rel_bias_table, ln1_g, ln1_b, w_gate, w_up, w_down, ln2_g, ln2_b):
    weights = _prepare_weights(a_w_qkv, a_q_gain, a_k_gain, a_w_o, b_w_qkv, b_sink, b_w_o,
                               rel_bias_table, ln1_g, ln1_b, w_gate, w_up, w_down, ln2_g, ln2_b)
    return (_trunk(x_prompt, *weights), _trunk(x_sample, *weights))
```

```python
import functools
import math

import jax
import jax.numpy as jnp
from jax import lax
from jax.experimental import pallas as pl
from jax.experimental.pallas import tpu as pltpu

D_MODEL = 1024
DEPTH = 2
N_HEADS = 8
N_KV_HEADS = 2
HEAD_DIM = 128
GQA_GROUP = N_HEADS // N_KV_HEADS
Q_DIM = N_HEADS * HEAD_DIM
KV_DIM = N_KV_HEADS * HEAD_DIM
QKV_DIM = Q_DIM + 2 * KV_DIM
D_FF = 2816
GRID_W = 64
AXIS_DIM = HEAD_DIM // 2
ROPE_THETA = 10000.0
WINDOW = 128
N_BUCKETS = 32
MAX_DISTANCE = 128
LN_EPS = 1e-5
RMS_EPS = 1e-6
DEEPNORM_ALPHA = (2.0 * DEPTH) ** 0.25
LOG2E = math.log2(math.e)
Q_PRESCALE = HEAD_DIM ** -0.5 * LOG2E
MASKED = -1e30
SAFE_LOG2_SPAN = 48.0
BF16_ROUNDING_SLACK = 1.02

ATTN_COLUMN_GROUP = 512

V7X_VMEM_LIMIT_BYTES = 56 * 1024 * 1024
V7X_F32_TILE = (8, 128)

BF16 = jnp.bfloat16
F32 = jnp.float32


def _params(sem):
    return pltpu.CompilerParams(dimension_semantics=sem, vmem_limit_bytes=V7X_VMEM_LIMIT_BYTES)


def _const_spec(shape):
    nd = len(shape)
    return pl.BlockSpec(shape, lambda *_: (0,) * nd, pipeline_mode=pl.Buffered(1))


HEADS_PER_DOT = 2


def _qkv_kernel(x_ref, wt_ref, *refs, norm_rope, bq):
    if norm_rope:
        cos_ref, sin_ref, gq_ref, gk_ref, qt_ref, k_ref, vt_ref = refs
    else:
        qt_ref, k_ref, vt_ref, norm_ref = refs
    tm = x_ref.shape[1]
    xb = x_ref[0].astype(BF16)
    max_sq = {"q": None, "k": None}

    def track_norm(name, slab):
        sq = jnp.sum(slab * slab, axis=0, keepdims=True)
        max_sq[name] = sq if max_sq[name] is None else jnp.maximum(max_sq[name], sq)

    def norm_rope_slab(y, gain_ref):
        r = lax.rsqrt(jnp.mean(y * y, axis=0, keepdims=True) + RMS_EPS)
        z = y * gain_ref[...]
        h = AXIS_DIM // 2
        partner = jnp.concatenate([z[h:2 * h], z[0:h], z[3 * h:4 * h], z[2 * h:3 * h]], axis=0)
        return (z * cos_ref[...] + partner * sin_ref[...]) * r

    rows = HEADS_PER_DOT * HEAD_DIM
    for grp in range(QKV_DIM // rows):
        yt = lax.dot_general(wt_ref[grp * rows:(grp + 1) * rows, :], xb, (((1,), (1,)), ((), ())),
                             preferred_element_type=F32)
        for sub in range(HEADS_PER_DOT):
            slab = yt[sub * HEAD_DIM:(sub + 1) * HEAD_DIM]
            head = grp * HEADS_PER_DOT + sub
            if head < N_HEADS:
                if norm_rope:
                    slab = norm_rope_slab(slab, gq_ref)
                else:
                    slab = slab * Q_PRESCALE
                    track_norm("q", slab)
                slab = slab.astype(BF16)
                kvh, g = divmod(head, GQA_GROUP)
                for j in range(tm // bq):
                    col = (j * GQA_GROUP + g) * bq
                    qt_ref[0, kvh, :, col:col + bq] = slab[:, j * bq:(j + 1) * bq]
            elif head < N_HEADS + N_KV_HEADS:
                kvh = head - N_HEADS
                if norm_rope:
                    slab = norm_rope_slab(slab, gk_ref)
                else:
                    track_norm("k", slab)
                k_ref[0, kvh] = slab.T.astype(BF16)
            else:
                kvh = head - N_HEADS - N_KV_HEADS
                vt_ref[0, kvh * HEAD_DIM:(kvh + 1) * HEAD_DIM, :] = slab.astype(BF16)
    if not norm_rope:
        row = lax.broadcasted_iota(jnp.int32, norm_ref.shape[2:], 0)
        norm_ref[0, 0] = jnp.where(row == 0, jnp.max(max_sq["q"], axis=1, keepdims=True),
                                   jnp.max(max_sq["k"], axis=1, keepdims=True))


def _qkv_project(x, wt, rope, *, bq, tm):
    B, S, _ = x.shape
    kern = functools.partial(_qkv_kernel, norm_rope=bool(rope), bq=bq)
    rope_specs = []
    if rope:
        table = pl.BlockSpec((HEAD_DIM, tm), lambda b, i: (0, i))
        rope_specs = [table, table, _const_spec((HEAD_DIM, 1)), _const_spec((HEAD_DIM, 1))]
    out_specs = [
        pl.BlockSpec((1, N_KV_HEADS, HEAD_DIM, GQA_GROUP * tm), lambda b, i: (b, 0, 0, i)),
        pl.BlockSpec((1, N_KV_HEADS, tm, HEAD_DIM), lambda b, i: (b, 0, i, 0)),
        pl.BlockSpec((1, KV_DIM, tm), lambda b, i: (b, 0, i)),
    ]
    out_shape = [
        jax.ShapeDtypeStruct((B, N_KV_HEADS, HEAD_DIM, GQA_GROUP * S), BF16),
        jax.ShapeDtypeStruct((B, N_KV_HEADS, S, HEAD_DIM), BF16),
        jax.ShapeDtypeStruct((B, KV_DIM, S), BF16),
    ]
    if not rope:
        out_specs.append(pl.BlockSpec((1, 1) + V7X_F32_TILE, lambda b, i: (b, i, 0, 0)))
        out_shape.append(jax.ShapeDtypeStruct((B, S // tm) + V7X_F32_TILE, F32))
    return pl.pallas_call(
        kern,
        grid=(B, S // tm),
        in_specs=[
            pl.BlockSpec((1, tm, D_MODEL), lambda b, i: (b, i, 0)),
            _const_spec((QKV_DIM, D_MODEL)),
        ] + rope_specs,
        out_specs=out_specs,
        out_shape=out_shape,
        compiler_params=_params(("parallel", "parallel")),
        name="qkv_project",
    )(x, wt, *rope)


def _store_heads(o_ref, out_t, bq):
    for g in range(GQA_GROUP):
        o_ref[0, :, g * HEAD_DIM:(g + 1) * HEAD_DIM] = out_t[:, g * bq:(g + 1) * bq].T.astype(BF16)


def _global_attn_kernel(qt_ref, k_ref, vt_ref, o_ref, acc_ref, *, bq, bkc, n_split):
    S = k_ref.shape[2]
    nq = GQA_GROUP * bq
    w = nq // n_split
    acc_ref[...] = jnp.zeros_like(acc_ref)

    def body(c, carry):
        ms, ls = carry
        start = pl.multiple_of(c * bkc, bkc)
        k = k_ref[0, 0, pl.ds(start, bkc), :]
        vt = vt_ref[0, :, pl.ds(start, bkc)]
        new_ms, new_ls = [], []
        for h in range(n_split):
            q = qt_ref[0, 0, :, h * w:(h + 1) * w]
            s = jnp.dot(k, q, preferred_element_type=F32)
            m_new = jnp.maximum(ms[h], jnp.max(s, axis=0, keepdims=True))
            alpha = jnp.exp2(ms[h] - m_new)
            p = jnp.exp2(s - m_new)
            new_ls.append(alpha * ls[h] + jnp.sum(p, axis=0, keepdims=True))
            new_ms.append(m_new)
            pv = jnp.dot(vt, p.astype(BF16), preferred_element_type=F32)
            acc_ref[:, h * w:(h + 1) * w] = alpha * acc_ref[:, h * w:(h + 1) * w] + pv
        return tuple(new_ms), tuple(new_ls)

    init = (tuple(jnp.full((1, w), MASKED, F32) for _ in range(n_split)),
            tuple(jnp.zeros((1, w), F32) for _ in range(n_split)))
    _, ls = lax.fori_loop(0, S // bkc, body, init)
    l = jnp.concatenate(ls, axis=1)
    _store_heads(o_ref, acc_ref[...] * (1.0 / l), bq)


def _global_attn_bounded_kernel(qt_ref, k_ref, vt_ref, o_ref, acc_ref, *, bq, bkc, n_split,
                                unroll):
    S = k_ref.shape[2]
    nq = GQA_GROUP * bq
    w = nq // n_split
    sub = V7X_F32_TILE[0]
    acc_ref[...] = jnp.zeros_like(acc_ref)

    def body(c, ls):
        start = pl.multiple_of(c * bkc, bkc)
        k = k_ref[0, 0, pl.ds(start, bkc), :]
        vt = vt_ref[0, :, pl.ds(start, bkc)]
        ss = [jnp.dot(k, qt_ref[0, 0, :, h * w:(h + 1) * w], preferred_element_type=F32)
              for h in range(n_split)]
        new_ls = []
        for h in range(n_split):
            cols = slice(h * w, (h + 1) * w)
            p = jnp.exp2(ss[h])
            new_ls.append(ls[h] + jnp.sum(p.reshape(bkc // sub, sub, w), axis=0))
            acc_ref[:, cols] += jnp.dot(vt, p.astype(BF16), preferred_element_type=F32)
        return tuple(new_ls)

    ls = lax.fori_loop(0, S // bkc, body, tuple(jnp.zeros((sub, w), F32) for _ in range(n_split)),
                       unroll=unroll)
    l = jnp.concatenate([jnp.sum(x, axis=0, keepdims=True) for x in ls], axis=1)
    _store_heads(o_ref, acc_ref[...] * (1.0 / l), bq)


def _global_attention_call(kern, name, qt, k, vt, *, bq):
    B, _, S, _ = k.shape
    nq = GQA_GROUP * bq
    return pl.pallas_call(
        kern,
        grid=(B, N_KV_HEADS, S // bq),
        in_specs=[
            pl.BlockSpec((1, 1, HEAD_DIM, nq), lambda b, h, i: (b, h, 0, i)),
            pl.BlockSpec((1, 1, S, HEAD_DIM), lambda b, h, i: (b, h, 0, 0)),
            pl.BlockSpec((1, HEAD_DIM, S), lambda b, h, i: (b, h, 0)),
        ],
        out_specs=pl.BlockSpec((1, bq, GQA_GROUP * HEAD_DIM), lambda b, h, i: (b, i, h)),
        out_shape=jax.ShapeDtypeStruct((B, S, Q_DIM), BF16),
        scratch_shapes=[pltpu.VMEM((HEAD_DIM, nq), F32)],
        compiler_params=_params(("parallel", "parallel", "arbitrary")),
        name=name,
    )(qt, k, vt)


def _global_attention_bounded(qt, k, vt, *, bq, bkc, n_split, unroll):
    kern = functools.partial(_global_attn_bounded_kernel, bq=bq, bkc=bkc, n_split=n_split,
                             unroll=unroll)
    return _global_attention_call(kern, "global_attention_bounded", qt, k, vt, bq=bq)


def _global_attention(qt, k, vt, *, bq, bkc, n_split):
    kern = functools.partial(_global_attn_kernel, bq=bq, bkc=bkc, n_split=n_split)
    return _global_attention_call(kern, "global_attention", qt, k, vt, bq=bq)


def _window_attn_kernel(qt_ref, k_ref, vt_ref, bias_ref, sink_ref, o_ref, s_ref, *, qb, bounded):
    step = pl.program_id(2)
    last_step = pl.num_programs(2) - 1
    nb = k_ref.shape[2] // WINDOW
    nq = GQA_GROUP * WINDOW
    sink = sink_ref[0]

    def block_start(n):
        return pl.multiple_of(n * WINDOW, WINDOW)

    starts = [[block_start(jnp.maximum(step * qb + qi - 1, 0)), block_start(step * qb + qi),
               block_start(jnp.minimum(step * qb + qi + 1, nb - 1))] for qi in range(qb)]
    for qi in range(qb):
        kw = jnp.concatenate([k_ref[0, 0, pl.ds(st, WINDOW), :] for st in starts[qi]], axis=0)
        s = jnp.dot(kw, qt_ref[0, 0, :, qi * nq:(qi + 1) * nq],
                    preferred_element_type=F32) + bias_ref[0]
        if qi == 0:
            s = jnp.concatenate(
                [jnp.where(step == 0, MASKED, s[:WINDOW]), s[WINDOW:]], axis=0)
        if qi == qb - 1:
            s = jnp.concatenate(
                [s[:2 * WINDOW], jnp.where(step == last_step, MASKED, s[2 * WINDOW:])], axis=0)
        s_ref[qi] = s
    for qi in range(qb):
        vw = jnp.concatenate([vt_ref[0, :, pl.ds(st, WINDOW)] for st in starts[qi]], axis=1)
        if bounded:
            p = jnp.exp2(s_ref[qi])
            l = jnp.sum(p, axis=0, keepdims=True) + jnp.exp2(sink)
        else:
            m = jnp.maximum(jnp.max(s_ref[qi], axis=0, keepdims=True), sink)
            p = jnp.exp2(s_ref[qi] - m)
            l = jnp.sum(p, axis=0, keepdims=True) + jnp.exp2(sink - m)
        out_t = jnp.dot(vw, p.astype(BF16), preferred_element_type=F32) * (1.0 / l)
        for g in range(GQA_GROUP):
            o_ref[0, qi * WINDOW:(qi + 1) * WINDOW, g * HEAD_DIM:(g + 1) * HEAD_DIM] = (
                out_t[:, g * WINDOW:(g + 1) * WINDOW].T.astype(BF16))


def _window_attention(qt, k, vt, bias_t, sink_t, *, qb, bounded):
    B, _, S, _ = k.shape
    nq = GQA_GROUP * WINDOW
    return pl.pallas_call(
        functools.partial(_window_attn_kernel, qb=qb, bounded=bounded),
        grid=(B, N_KV_HEADS, S // (qb * WINDOW)),
        in_specs=[
            pl.BlockSpec((1, 1, HEAD_DIM, qb * nq), lambda b, h, n: (b, h, 0, n)),
            pl.BlockSpec((1, 1, S, HEAD_DIM), lambda b, h, n: (b, h, 0, 0)),
            pl.BlockSpec((1, HEAD_DIM, S), lambda b, h, n: (b, h, 0)),
            pl.BlockSpec((1, 3 * WINDOW, nq), lambda b, h, n: (h, 0, 0)),
            pl.BlockSpec((1, 1, nq), lambda b, h, n: (h, 0, 0)),
        ],
        out_specs=pl.BlockSpec((1, qb * WINDOW, GQA_GROUP * HEAD_DIM), lambda b, h, n: (b, n, h)),
        out_shape=jax.ShapeDtypeStruct((B, S, Q_DIM), BF16),
        scratch_shapes=[pltpu.VMEM((qb, 3 * WINDOW, nq), F32)],
        compiler_params=_params(("parallel", "parallel", "parallel")),
        name="window_attention",
    )(qt, k, vt, bias_t, sink_t)


def _layer_norm(y, g, b):
    mu = jnp.mean(y, axis=-1, keepdims=True)
    d = y - mu
    var = jnp.mean(d * d, axis=-1, keepdims=True)
    return d * lax.rsqrt(var + LN_EPS) * g + b


ROW_GROUPS = 4


def _post_attention_kernel(x_ref, o_ref, wo_ref, g1_ref, b1_ref, wg_ref, wu_ref, wd_ref, g2_ref,
                           b2_ref, y_ref):
    tm = x_ref.shape[0]
    rows = [slice(r * tm // ROW_GROUPS, (r + 1) * tm // ROW_GROUPS) for r in range(ROW_GROUPS)]
    hs = [jnp.dot(o_ref[r, :], wo_ref[...], preferred_element_type=F32) for r in rows]
    x1s = [_layer_norm(DEEPNORM_ALPHA * x_ref[r, :] + h, g1_ref[...], b1_ref[...])
           for r, h in zip(rows, hs)]
    mids = []
    for x1 in x1s:
        xb = x1.astype(BF16)
        gate = jnp.dot(xb, wg_ref[...], preferred_element_type=F32)
        up = jnp.dot(xb, wu_ref[...], preferred_element_type=F32)
        mids.append((gate * jax.nn.sigmoid(gate) * up).astype(BF16))
    hs = [jnp.dot(mid, wd_ref[...], preferred_element_type=F32) for mid in mids]
    for r, x1, h in zip(rows, x1s, hs):
        y_ref[r, :] = _layer_norm(DEEPNORM_ALPHA * x1 + h, g2_ref[...], b2_ref[...])


def _post_attention(x, o, wo, g1, b1, wg, wu, wd, g2, b2, *, layer, tm):
    T = x.shape[0]
    vec = _const_spec((1, D_MODEL))

    def stacked(shape):
        return pl.BlockSpec((None,) + shape, lambda t: (layer, 0, 0), pipeline_mode=pl.Buffered(1))

    return pl.pallas_call(
        _post_attention_kernel,
        grid=(T // tm,),
        in_specs=[
            pl.BlockSpec((tm, D_MODEL), lambda i: (i, 0)),
            pl.BlockSpec((tm, Q_DIM), lambda i: (i, 0)),
            _const_spec((Q_DIM, D_MODEL)), vec, vec,
            stacked((D_MODEL, D_FF)),
            stacked((D_MODEL, D_FF)),
            stacked((D_FF, D_MODEL)), vec, vec,
        ],
        out_specs=pl.BlockSpec((tm, D_MODEL), lambda i: (i, 0)),
        out_shape=jax.ShapeDtypeStruct((T, D_MODEL), F32),
        compiler_params=_params(("parallel",)),
        name="post_attention",
    )(x, o, wo, g1, b1, wg, wu, wd, g2, b2)


def _rope_tables_t(seq_len):
    rows_n = seq_len // GRID_W
    inv_freq = ROPE_THETA ** (-jnp.arange(0, AXIS_DIM, 2, dtype=F32) / AXIS_DIM)
    ang_r = (jnp.arange(rows_n, dtype=F32)[:, None] * inv_freq).T
    ang_c = (jnp.arange(GRID_W, dtype=F32)[:, None] * inv_freq).T
    over_rows = lambda a: jnp.repeat(a, GRID_W, axis=1)
    over_cols = lambda a: jnp.tile(a, (1, rows_n))
    cos_r, sin_r = over_rows(jnp.cos(ang_r)), over_rows(jnp.sin(ang_r))
    cos_c, sin_c = over_cols(jnp.cos(ang_c)), over_cols(jnp.sin(ang_c))
    cos_t = jnp.concatenate([cos_r, cos_r, cos_c, cos_c], axis=0)
    sin_t = jnp.concatenate([-sin_r, sin_r, -sin_c, sin_c], axis=0)
    return cos_t, sin_t


def _t5_bucket(rel):
    nb = N_BUCKETS // 2
    max_exact = nb // 2
    base = (rel > 0).astype(jnp.int32) * nb
    n = jnp.abs(rel)
    nf = jnp.maximum(n, max_exact).astype(F32)
    large = max_exact + (jnp.log(nf / max_exact) / math.log(MAX_DISTANCE / max_exact)
                         * (nb - max_exact)).astype(jnp.int32)
    large = jnp.minimum(large, nb - 1)
    return base + jnp.where(n < max_exact, n, large)


def _window_bias_t(rel_bias_table):
    C = 3 * WINDOW
    n_diag = C + WINDOW - 1
    rel = jnp.arange(n_diag + 1) - (2 * WINDOW - 1)
    bucket = _t5_bucket(rel)[:, None]
    table = rel_bias_table.astype(F32)
    f = sum(jnp.where(bucket == b, table[b], 0.0) for b in range(N_BUCKETS)) * LOG2E
    f = jnp.where((jnp.abs(rel) <= WINDOW)[:, None], f, MASKED)
    skew = jnp.tile(f, (WINDOW, 1))[:WINDOW * n_diag].reshape(WINDOW, n_diag, N_HEADS)
    bias = skew[:, WINDOW - 1:WINDOW - 1 + C]
    bias = bias.transpose(2, 1, 0).reshape(N_KV_HEADS, GQA_GROUP, C, WINDOW)
    return bias.transpose(0, 2, 1, 3).reshape(N_KV_HEADS, C, GQA_GROUP * WINDOW)


def _trunk(x, a_wt, a_q_gain, a_k_gain, a_wo, b_wt, b_sink, b_wo, rel_bias_table,
           ln1_g, ln1_b, wg, wu, wd, ln2_g, ln2_b):
    B, S, _ = x.shape
    T = B * S
    tm_qkv = min(2048, S)
    tm_tok = min(1024, T)
    bq = min(1024, S)
    bkc = min(1024, S)
    qb = min(16, S // WINDOW)
    cos_t, sin_t = _rope_tables_t(S)
    for i in range(DEPTH):
        j = i // 2
        if i % 2 == 0:
            gq = a_q_gain[j].astype(F32) * Q_PRESCALE
            gk = a_k_gain[j].astype(F32)
            qt, k, vt = _qkv_project(x, a_wt[j], (cos_t, sin_t, gq[:, None], gk[:, None]),
                                     bq=bq, tm=tm_qkv)
            bound = HEAD_DIM * jnp.max(jnp.abs(gq)) * jnp.max(jnp.abs(gk)) * BF16_ROUNDING_SLACK
            n_split = GQA_GROUP * bq // ATTN_COLUMN_GROUP
            attn = functools.partial(_global_attention, bq=bq, bkc=512, n_split=n_split)
            attn_bounded = functools.partial(_global_attention_bounded, bq=bq, bkc=bkc,
                                             n_split=n_split, unroll=4)
            o = lax.cond(bound <= SAFE_LOG2_SPAN, attn_bounded, attn, qt, k, vt)
            wo = a_wo[j]
        else:
            qt, k, vt, norms = _qkv_project(x, b_wt[j], (), bq=WINDOW, tm=tm_qkv)
            bias_t = _window_bias_t(rel_bias_table)
            sink_t = jnp.repeat(b_sink[j].astype(F32) * LOG2E, WINDOW).reshape(
                N_KV_HEADS, 1, GQA_GROUP * WINDOW)
            qk_bound = jnp.sqrt(jnp.max(norms[:, :, 0]) * jnp.max(norms[:, :, 1])) * BF16_ROUNDING_SLACK
            bias_bound = jnp.max(jnp.where(bias_t > MASKED, jnp.abs(bias_t), 0.0))
            bounded = ((qk_bound + bias_bound <= SAFE_LOG2_SPAN)
                       & (jnp.max(jnp.abs(sink_t)) <= SAFE_LOG2_SPAN))
            window = functools.partial(_window_attention, qb=qb)
            o = lax.cond(bounded, functools.partial(window, bounded=True),
                         functools.partial(window, bounded=False), qt, k, vt, bias_t, sink_t)
            wo = b_wo[j]
        x2 = _post_attention(x.reshape(T, D_MODEL), o.reshape(T, Q_DIM), wo,
                             ln1_g[i][None], ln1_b[i][None], wg, wu, wd,
                             ln2_g[i][None], ln2_b[i][None], layer=i, tm=tm_tok)
        x = x2.reshape(B, S, D_MODEL)
    return x


def _prepare_weights(a_w_qkv, a_q_gain, a_k_gain, a_w_o, b_w_qkv, b_sink, b_w_o, rel_bias_table,
                     ln1_g, ln1_b, w_gate, w_up, w_down, ln2_g, ln2_b):
    return (jnp.swapaxes(a_w_qkv.astype(BF16), 1, 2), a_q_gain, a_k_gain, a_w_o.astype(BF16),
            jnp.swapaxes(b_w_qkv.astype(BF16), 1, 2), b_sink, b_w_o.astype(BF16),
            rel_bias_table, ln1_g, ln1_b, w_gate.astype(BF16), w_up.astype(BF16),
            w_down.astype(BF16), ln2_g, ln2_b)


def kernel(x_prompt, x_sample, a_w_qkv, a_q_gain, a_k_gain, a_w_o, b_w_qkv, b_sink, b_w_o,
           rel_bias_table, ln1_g, ln1_b, w_gate, w_up, w_down, ln2_g, ln2_b):
    weights = _prepare_weights(a_w_qkv, a_q_gain, a_k_gain, a_w_o, b_w_qkv, b_sink, b_w_o,
                               rel_bias_table, ln1_g, ln1_b, w_gate, w_up, w_down, ln2_g, ln2_b)
    return (_trunk(x_prompt, *weights), _trunk(x_sample, *weights))
```

```python
import functools
import math

import jax
import jax.numpy as jnp
from jax import lax
from jax.experimental import pallas as pl
from jax.experimental.pallas import tpu as pltpu

D_MODEL = 1024
DEPTH = 2
N_HEADS = 8
N_KV_HEADS = 2
HEAD_DIM = 128
GQA_GROUP = N_HEADS // N_KV_HEADS
Q_DIM = N_HEADS * HEAD_DIM
KV_DIM = N_KV_HEADS * HEAD_DIM
QKV_DIM = Q_DIM + 2 * KV_DIM
D_FF = 2816
GRID_W = 64
AXIS_DIM = HEAD_DIM // 2
ROPE_THETA = 10000.0
WINDOW = 128
N_BUCKETS = 32
MAX_DISTANCE = 128
LN_EPS = 1e-5
RMS_EPS = 1e-6
DEEPNORM_ALPHA = (2.0 * DEPTH) ** 0.25
LOG2E = math.log2(math.e)
Q_PRESCALE = HEAD_DIM ** -0.5 * LOG2E
MASKED = -1e30
SAFE_LOG2_SPAN = 48.0
BF16_ROUNDING_SLACK = 1.02

ATTN_COLUMN_GROUP = 512

V7X_VMEM_LIMIT_BYTES = 56 * 1024 * 1024
V7X_F32_TILE = (8, 128)

BF16 = jnp.bfloat16
F32 = jnp.float32


def _params(sem):
    return pltpu.CompilerParams(dimension_semantics=sem, vmem_limit_bytes=V7X_VMEM_LIMIT_BYTES)


def _const_spec(shape):
    nd = len(shape)
    return pl.BlockSpec(shape, lambda *_: (0,) * nd, pipeline_mode=pl.Buffered(1))


HEADS_PER_DOT = 2


def _qkv_kernel(x_ref, wt_ref, *refs, norm_rope, bq):
    if norm_rope:
        cos_ref, sin_ref, gq_ref, gk_ref, qt_ref, k_ref, vt_ref = refs
    else:
        qt_ref, k_ref, vt_ref, norm_ref = refs
    tm = x_ref.shape[1]
    xb = x_ref[0].astype(BF16)
    max_sq = {"q": None, "k": None}

    def track_norm(name, slab):
        sq = jnp.sum(slab * slab, axis=0, keepdims=True)
        max_sq[name] = sq if max_sq[name] is None else jnp.maximum(max_sq[name], sq)

    def norm_rope_slab(y, gain_ref):
        r = lax.rsqrt(jnp.mean(y * y, axis=0, keepdims=True) + RMS_EPS)
        z = y * gain_ref[...]
        h = AXIS_DIM // 2
        partner = jnp.concatenate([z[h:2 * h], z[0:h], z[3 * h:4 * h], z[2 * h:3 * h]], axis=0)
        return (z * cos_ref[...] + partner * sin_ref[...]) * r

    rows = HEADS_PER_DOT * HEAD_DIM
    for grp in range(QKV_DIM // rows):
        yt = lax.dot_general(wt_ref[grp * rows:(grp + 1) * rows, :], xb, (((1,), (1,)), ((), ())),
                             preferred_element_type=F32)
        for sub in range(HEADS_PER_DOT):
            slab = yt[sub * HEAD_DIM:(sub + 1) * HEAD_DIM]
            head = grp * HEADS_PER_DOT + sub
            if head < N_HEADS:
                if norm_rope:
                    slab = norm_rope_slab(slab, gq_ref)
                else:
                    slab = slab * Q_PRESCALE
                    track_norm("q", slab)
                slab = slab.astype(BF16)
                kvh, g = divmod(head, GQA_GROUP)
                for j in range(tm // bq):
                    col = (j * GQA_GROUP + g) * bq
                    qt_ref[0, kvh, :, col:col + bq] = slab[:, j * bq:(j + 1) * bq]
            elif head < N_HEADS + N_KV_HEADS:
                kvh = head - N_HEADS
                if norm_rope:
                    slab = norm_rope_slab(slab, gk_ref)
                else:
                    track_norm("k", slab)
                k_ref[0, kvh] = slab.T.astype(BF16)
            else:
                kvh = head - N_HEADS - N_KV_HEADS
                vt_ref[0, kvh * HEAD_DIM:(kvh + 1) * HEAD_DIM, :] = slab.astype(BF16)
    if not norm_rope:
        row = lax.broadcasted_iota(jnp.int32, norm_ref.shape[2:], 0)
        norm_ref[0, 0] = jnp.where(row == 0, jnp.max(max_sq["q"], axis=1, keepdims=True),
                                   jnp.max(max_sq["k"], axis=1, keepdims=True))


def _qkv_project(x, wt, rope, *, bq, tm):
    B, S, _ = x.shape
    kern = functools.partial(_qkv_kernel, norm_rope=bool(rope), bq=bq)
    rope_specs = []
    if rope:
        table = pl.BlockSpec((HEAD_DIM, tm), lambda b, i: (0, i))
        rope_specs = [table, table, _const_spec((HEAD_DIM, 1)), _const_spec((HEAD_DIM, 1))]
    out_specs = [
        pl.BlockSpec((1, N_KV_HEADS, HEAD_DIM, GQA_GROUP * tm), lambda b, i: (b, 0, 0, i)),
        pl.BlockSpec((1, N_KV_HEADS, tm, HEAD_DIM), lambda b, i: (b, 0, i, 0)),
        pl.BlockSpec((1, KV_DIM, tm), lambda b, i: (b, 0, i)),
    ]
    out_shape = [
        jax.ShapeDtypeStruct((B, N_KV_HEADS, HEAD_DIM, GQA_GROUP * S), BF16),
        jax.ShapeDtypeStruct((B, N_KV_HEADS, S, HEAD_DIM), BF16),
        jax.ShapeDtypeStruct((B, KV_DIM, S), BF16),
    ]
    if not rope:
        out_specs.append(pl.BlockSpec((1, 1) + V7X_F32_TILE, lambda b, i: (b, i, 0, 0)))
        out_shape.append(jax.ShapeDtypeStruct((B, S // tm) + V7X_F32_TILE, F32))
    return pl.pallas_call(
        kern,
        grid=(B, S // tm),
        in_specs=[
            pl.BlockSpec((1, tm, D_MODEL), lambda b, i: (b, i, 0)),
            _const_spec((QKV_DIM, D_MODEL)),
        ] + rope_specs,
        out_specs=out_specs,
        out_shape=out_shape,
        compiler_params=_params(("parallel", "parallel")),
        name="qkv_project",
    )(x, wt, *rope)


def _store_heads(o_ref, out_t, bq):
    for g in range(GQA_GROUP):
        o_ref[0, :, g * HEAD_DIM:(g + 1) * HEAD_DIM] = out_t[:, g * bq:(g + 1) * bq].T.astype(BF16)


def _global_attn_kernel(qt_ref, k_ref, vt_ref, o_ref, acc_ref, *, bq, bkc, n_split):
    S = k_ref.shape[2]
    nq = GQA_GROUP * bq
    w = nq // n_split
    acc_ref[...] = jnp.zeros_like(acc_ref)

    def body(c, carry):
        ms, ls = carry
        start = pl.multiple_of(c * bkc, bkc)
        k = k_ref[0, 0, pl.ds(start, bkc), :]
        vt = vt_ref[0, :, pl.ds(start, bkc)]
        new_ms, new_ls = [], []
        for h in range(n_split):
            q = qt_ref[0, 0, :, h * w:(h + 1) * w]
            s = jnp.dot(k, q, preferred_element_type=F32)
            m_new = jnp.maximum(ms[h], jnp.max(s, axis=0, keepdims=True))
            alpha = jnp.exp2(ms[h] - m_new)
            p = jnp.exp2(s - m_new)
            new_ls.append(alpha * ls[h] + jnp.sum(p, axis=0, keepdims=True))
            new_ms.append(m_new)
            pv = jnp.dot(vt, p.astype(BF16), preferred_element_type=F32)
            acc_ref[:, h * w:(h + 1) * w] = alpha * acc_ref[:, h * w:(h + 1) * w] + pv
        return tuple(new_ms), tuple(new_ls)

    init = (tuple(jnp.full((1, w), MASKED, F32) for _ in range(n_split)),
            tuple(jnp.zeros((1, w), F32) for _ in range(n_split)))
    _, ls = lax.fori_loop(0, S // bkc, body, init)
    l = jnp.concatenate(ls, axis=1)
    _store_heads(o_ref, acc_ref[...] * (1.0 / l), bq)


def _global_attn_bounded_kernel(qt_ref, k_ref, vt_ref, o_ref, acc_ref, *, bq, bkc, n_split,
                                unroll):
    S = k_ref.shape[2]
    nq = GQA_GROUP * bq
    w = nq // n_split
    sub = V7X_F32_TILE[0]
    acc_ref[...] = jnp.zeros_like(acc_ref)

    def body(c, ls):
        start = pl.multiple_of(c * bkc, bkc)
        k = k_ref[0, 0, pl.ds(start, bkc), :]
        vt = vt_ref[0, :, pl.ds(start, bkc)]
        ss = [jnp.dot(k, qt_ref[0, 0, :, h * w:(h + 1) * w], preferred_element_type=F32)
              for h in range(n_split)]
        new_ls = []
        for h in range(n_split):
            cols = slice(h * w, (h + 1) * w)
            p = jnp.exp2(ss[h])
            new_ls.append(ls[h] + jnp.sum(p.reshape(bkc // sub, sub, w), axis=0))
            acc_ref[:, cols] += jnp.dot(vt, p.astype(BF16), preferred_element_type=F32)
        return tuple(new_ls)

    ls = lax.fori_loop(0, S // bkc, body, tuple(jnp.zeros((sub, w), F32) for _ in range(n_split)),
                       unroll=unroll)
    l = jnp.concatenate([jnp.sum(x, axis=0, keepdims=True) for x in ls], axis=1)
    _store_heads(o_ref, acc_ref[...] * (1.0 / l), bq)


def _global_attention_call(kern, name, qt, k, vt, *, bq):
    B, _, S, _ = k.shape
    nq = GQA_GROUP * bq
    return pl.pallas_call(
        kern,
        grid=(B, N_KV_HEADS, S // bq),
        in_specs=[
            pl.BlockSpec((1, 1, HEAD_DIM, nq), lambda b, h, i: (b, h, 0, i)),
            pl.BlockSpec((1, 1, S, HEAD_DIM), lambda b, h, i: (b, h, 0, 0)),
            pl.BlockSpec((1, HEAD_DIM, S), lambda b, h, i: (b, h, 0)),
        ],
        out_specs=pl.BlockSpec((1, bq, GQA_GROUP * HEAD_DIM), lambda b, h, i: (b, i, h)),
        out_shape=jax.ShapeDtypeStruct((B, S, Q_DIM), BF16),
        scratch_shapes=[pltpu.VMEM((HEAD_DIM, nq), F32)],
        compiler_params=_params(("parallel", "parallel", "arbitrary")),
        name=name,
    )(qt, k, vt)


def _global_attention_bounded(qt, k, vt, *, bq, bkc, n_split, unroll):
    kern = functools.partial(_global_attn_bounded_kernel, bq=bq, bkc=bkc, n_split=n_split,
                             unroll=unroll)
    return _global_attention_call(kern, "global_attention_bounded", qt, k, vt, bq=bq)


def _global_attention(qt, k, vt, *, bq, bkc, n_split):
    kern = functools.partial(_global_attn_kernel, bq=bq, bkc=bkc, n_split=n_split)
    return _global_attention_call(kern, "global_attention", qt, k, vt, bq=bq)


def _window_attn_kernel(qt_ref, k_ref, vt_ref, bias_ref, sink_ref, o_ref, s_ref, *, qb, bounded):
    step = pl.program_id(2)
    last_step = pl.num_programs(2) - 1
    nb = k_ref.shape[2] // WINDOW
    nq = GQA_GROUP * WINDOW
    sink = sink_ref[0]

    def block_start(n):
        return pl.multiple_of(n * WINDOW, WINDOW)

    starts = [[block_start(jnp.maximum(step * qb + qi - 1, 0)), block_start(step * qb + qi),
               block_start(jnp.minimum(step * qb + qi + 1, nb - 1))] for qi in range(qb)]
    for qi in range(qb):
        kw = jnp.concatenate([k_ref[0, 0, pl.ds(st, WINDOW), :] for st in starts[qi]], axis=0)
        s = jnp.dot(kw, qt_ref[0, 0, :, qi * nq:(qi + 1) * nq],
                    preferred_element_type=F32) + bias_ref[0]
        if qi == 0:
            s = jnp.concatenate(
                [jnp.where(step == 0, MASKED, s[:WINDOW]), s[WINDOW:]], axis=0)
        if qi == qb - 1:
            s = jnp.concatenate(
                [s[:2 * WINDOW], jnp.where(step == last_step, MASKED, s[2 * WINDOW:])], axis=0)
        s_ref[qi] = s
    for qi in range(qb):
        vw = jnp.concatenate([vt_ref[0, :, pl.ds(st, WINDOW)] for st in starts[qi]], axis=1)
        if bounded:
            p = jnp.exp2(s_ref[qi])
            l = jnp.sum(p, axis=0, keepdims=True) + jnp.exp2(sink)
        else:
            m = jnp.maximum(jnp.max(s_ref[qi], axis=0, keepdims=True), sink)
            p = jnp.exp2(s_ref[qi] - m)
            l = jnp.sum(p, axis=0, keepdims=True) + jnp.exp2(sink - m)
        out_t = jnp.dot(vw, p.astype(BF16), preferred_element_type=F32) * (1.0 / l)
        for g in range(GQA_GROUP):
            o_ref[0, qi * WINDOW:(qi + 1) * WINDOW, g * HEAD_DIM:(g + 1) * HEAD_DIM] = (
                out_t[:, g * WINDOW:(g + 1) * WINDOW].T.astype(BF16))


def _window_attention(qt, k, vt, bias_t, sink_t, *, qb, bounded):
    B, _, S, _ = k.shape
    nq = GQA_GROUP * WINDOW
    return pl.pallas_call(
        functools.partial(_window_attn_kernel, qb=qb, bounded=bounded),
        grid=(B, N_KV_HEADS, S // (qb * WINDOW)),
        in_specs=[
            pl.BlockSpec((1, 1, HEAD_DIM, qb * nq), lambda b, h, n: (b, h, 0, n)),
            pl.BlockSpec((1, 1, S, HEAD_DIM), lambda b, h, n: (b, h, 0, 0)),
            pl.BlockSpec((1, HEAD_DIM, S), lambda b, h, n: (b, h, 0)),
            pl.BlockSpec((1, 3 * WINDOW, nq), lambda b, h, n: (h, 0, 0)),
            pl.BlockSpec((1, 1, nq), lambda b, h, n: (h, 0, 0)),
        ],
        out_specs=pl.BlockSpec((1, qb * WINDOW, GQA_GROUP * HEAD_DIM), lambda b, h, n: (b, n, h)),
        out_shape=jax.ShapeDtypeStruct((B, S, Q_DIM), BF16),
        scratch_shapes=[pltpu.VMEM((qb, 3 * WINDOW, nq), F32)],
        compiler_params=_params(("parallel", "parallel", "parallel")),
        name="window_attention",
    )(qt, k, vt, bias_t, sink_t)


def _layer_norm(y, g, b):
    mu = jnp.mean(y, axis=-1, keepdims=True)
    d = y - mu
    var = jnp.mean(d * d, axis=-1, keepdims=True)
    return d * lax.rsqrt(var + LN_EPS) * g + b


ROW_GROUPS = 4


def _post_attention_kernel(x_ref, o_ref, wo_ref, g1_ref, b1_ref, wg_ref, wu_ref, wd_ref, g2_ref,
                           b2_ref, y_ref):
    tm = x_ref.shape[0]
    rows = [slice(r * tm // ROW_GROUPS, (r + 1) * tm // ROW_GROUPS) for r in range(ROW_GROUPS)]
    hs = [jnp.dot(o_ref[r, :], wo_ref[...], preferred_element_type=F32) for r in rows]
    x1s = [_layer_norm(DEEPNORM_ALPHA * x_ref[r, :] + h, g1_ref[...], b1_ref[...])
           for r, h in zip(rows, hs)]
    mids = []
    for x1 in x1s:
        xb = x1.astype(BF16)
        gate = jnp.dot(xb, wg_ref[...], preferred_element_type=F32)
        up = jnp.dot(xb, wu_ref[...], preferred_element_type=F32)
        mids.append((gate * jax.nn.sigmoid(gate) * up).astype(BF16))
    hs = [jnp.dot(mid, wd_ref[...], preferred_element_type=F32) for mid in mids]
    for r, x1, h in zip(rows, x1s, hs):
        y_ref[r, :] = _layer_norm(DEEPNORM_ALPHA * x1 + h, g2_ref[...], b2_ref[...])


def _post_attention(x, o, wo, g1, b1, wg, wu, wd, g2, b2, *, layer, tm):
    T = x.shape[0]
    vec = _const_spec((1, D_MODEL))

    def stacked(shape):
        return pl.BlockSpec((None,) + shape, lambda t: (layer, 0, 0), pipeline_mode=pl.Buffered(1))

    return pl.pallas_call(
        _post_attention_kernel,
        grid=(T // tm,),
        in_specs=[
            pl.BlockSpec((tm, D_MODEL), lambda i: (i, 0)),
            pl.BlockSpec((tm, Q_DIM), lambda i: (i, 0)),
            _const_spec((Q_DIM, D_MODEL)), vec, vec,
            stacked((D_MODEL, D_FF)),
            stacked((D_MODEL, D_FF)),
            stacked((D_FF, D_MODEL)), vec, vec,
        ],
        out_specs=pl.BlockSpec((tm, D_MODEL), lambda i: (i, 0)),
        out_shape=jax.ShapeDtypeStruct((T, D_MODEL), F32),
        compiler_params=_params(("parallel",)),
        name="post_attention",
    )(x, o, wo, g1, b1, wg, wu, wd, g2, b2)


def _rope_tables_t(seq_len):
    rows_n = seq_len // GRID_W
    inv_freq = ROPE_THETA ** (-jnp.arange(0, AXIS_DIM, 2, dtype=F32) / AXIS_DIM)
    ang_r = (jnp.arange(rows_n, dtype=F32)[:, None] * inv_freq).T
    ang_c = (jnp.arange(GRID_W, dtype=F32)[:, None] * inv_freq).T
    over_rows = lambda a: jnp.repeat(a, GRID_W, axis=1)
    over_cols = lambda a: jnp.tile(a, (1, rows_n))
    cos_r, sin_r = over_rows(jnp.cos(ang_r)), over_rows(jnp.sin(ang_r))
    cos_c, sin_c = over_cols(jnp.cos(ang_c)), over_cols(jnp.sin(ang_c))
    cos_t = jnp.concatenate([cos_r, cos_r, cos_c, cos_c], axis=0)
    sin_t = jnp.concatenate([-sin_r, sin_r, -sin_c, sin_c], axis=0)
    return cos_t, sin_t


def _t5_bucket(rel):
    nb = N_BUCKETS // 2
    max_exact = nb // 2
    base = (rel > 0).astype(jnp.int32) * nb
    n = jnp.abs(rel)
    nf = jnp.maximum(n, max_exact).astype(F32)
    large = max_exact + (jnp.log(nf / max_exact) / math.log(MAX_DISTANCE / max_exact)
                         * (nb - max_exact)).astype(jnp.int32)
    large = jnp.minimum(large, nb - 1)
    return base + jnp.where(n < max_exact, n, large)


def _window_bias_t(rel_bias_table):
    C = 3 * WINDOW
    n_diag = C + WINDOW - 1
    rel = jnp.arange(n_diag + 1) - (2 * WINDOW - 1)
    bucket = _t5_bucket(rel)[:, None]
    table = rel_bias_table.astype(F32)
    f = sum(jnp.where(bucket == b, table[b], 0.0) for b in range(N_BUCKETS)) * LOG2E
    f = jnp.where((jnp.abs(rel) <= WINDOW)[:, None], f, MASKED)
    skew = jnp.tile(f, (WINDOW, 1))[:WINDOW * n_diag].reshape(WINDOW, n_diag, N_HEADS)
    bias = skew[:, WINDOW - 1:WINDOW - 1 + C]
    bias = bias.transpose(2, 1, 0).reshape(N_KV_HEADS, GQA_GROUP, C, WINDOW)
    return bias.transpose(0, 2, 1, 3).reshape(N_KV_HEADS, C, GQA_GROUP * WINDOW)


def _trunk(x, a_wt, a_q_gain, a_k_gain, a_wo, b_wt, b_sink, b_wo, rel_bias_table,
           ln1_g, ln1_b, wg, wu, wd, ln2_g, ln2_b):
    B, S, _ = x.shape
    T = B * S
    tm_qkv = min(2048, S)
    tm_tok = min(1024, T)
    bq = S if S <= 2048 else 1024
    bkc = min(1024, S)
    qb = min(16, S // WINDOW)
    cos_t, sin_t = _rope_tables_t(S)
    for i in range(DEPTH):
        j = i // 2
        if i % 2 == 0:
            gq = a_q_gain[j].astype(F32) * Q_PRESCALE
            gk = a_k_gain[j].astype(F32)
            qt, k, vt = _qkv_project(x, a_wt[j], (cos_t, sin_t, gq[:, None], gk[:, None]),
                                     bq=bq, tm=tm_qkv)
            bound = HEAD_DIM * jnp.max(jnp.abs(gq)) * jnp.max(jnp.abs(gk)) * BF16_ROUNDING_SLACK
            n_split = GQA_GROUP * bq // ATTN_COLUMN_GROUP
            attn = functools.partial(_global_attention, bq=bq, bkc=512, n_split=n_split)
            attn_bounded = functools.partial(_global_attention_bounded, bq=bq, bkc=bkc,
                                             n_split=n_split, unroll=4)
            o = lax.cond(bound <= SAFE_LOG2_SPAN, attn_bounded, attn, qt, k, vt)
            wo = a_wo[j]
        else:
            qt, k, vt, norms = _qkv_project(x, b_wt[j], (), bq=WINDOW, tm=tm_qkv)
            bias_t = _window_bias_t(rel_bias_table)
            sink_t = jnp.repeat(b_sink[j].astype(F32) * LOG2E, WINDOW).reshape(
                N_KV_HEADS, 1, GQA_GROUP * WINDOW)
            qk_bound = jnp.sqrt(jnp.max(norms[:, :, 0]) * jnp.max(norms[:, :, 1])) * BF16_ROUNDING_SLACK
            bias_bound = jnp.max(jnp.where(bias_t > MASKED, jnp.abs(bias_t), 0.0))
            bounded = ((qk_bound + bias_bound <= SAFE_LOG2_SPAN)
                       & (jnp.max(jnp.abs(sink_t)) <= SAFE_LOG2_SPAN))
            window = functools.partial(_window_attention, qb=qb)
            o = lax.cond(bounded, functools.partial(window, bounded=True),
                         functools.partial(window, bounded=False), qt, k, vt, bias_t, sink_t)
            wo = b_wo[j]
        x2 = _post_attention(x.reshape(T, D_MODEL), o.reshape(T, Q_DIM), wo,
                             ln1_g[i][None], ln1_b[i][None], wg, wu, wd,
                             ln2_g[i][None], ln2_b[i][None], layer=i, tm=tm_tok)
        x = x2.reshape(B, S, D_MODEL)
    return x


def _prepare_weights(a_w_qkv, a_q_gain, a_k_gain, a_w_o, b_w_qkv, b_sink, b_w_o, rel_bias_table,
                     ln1_g, ln1_b, w_gate, w_up, w_down, ln2_g, ln2_b):
    return (jnp.swapaxes(a_w_qkv.astype(BF16), 1, 2), a_q_gain, a_k_gain, a_w_o.astype(BF16),
            jnp.swapaxes(b_w_qkv.astype(BF16), 1, 2), b_sink, b_w_o.astype(BF16),
            rel_bias_table, ln1_g, ln1_b, w_gate.astype(BF16), w_up.astype(BF16),
            w_down.astype(BF16), ln2_g, ln2_b)


def kernel(x_prompt, x_sample, a_w_qkv, a_q_gain, a_k_gain, a_w_o, b_w_qkv, b_sink, b_w_o,
           rel_bias_table, ln1_g, ln1_b, w_gate, w_up, w_down, ln2_g, ln2_b):
    weights = _prepare_weights(a_w_qkv, a_q_gain, a_k_gain, a_w_o, b_w_qkv, b_sink, b_w_o,
                               rel_bias_table, ln1_g, ln1_b, w_gate, w_up, w_down, ln2_g, ln2_b)
    return (_trunk(x_prompt, *weights), _trunk(x_sample, *weights))
```

```python
import functools
import math

import jax
import jax.numpy as jnp
from jax import lax
from jax.experimental import pallas as pl
from jax.experimental.pallas import tpu as pltpu

D_MODEL = 1024
DEPTH = 2
N_HEADS = 8
N_KV_HEADS = 2
HEAD_DIM = 128
GQA_GROUP = N_HEADS // N_KV_HEADS
Q_DIM = N_HEADS * HEAD_DIM
KV_DIM = N_KV_HEADS * HEAD_DIM
QKV_DIM = Q_DIM + 2 * KV_DIM
D_FF = 2816
GRID_W = 64
AXIS_DIM = HEAD_DIM // 2
ROPE_THETA = 10000.0
WINDOW = 128
N_BUCKETS = 32
MAX_DISTANCE = 128
LN_EPS = 1e-5
RMS_EPS = 1e-6
DEEPNORM_ALPHA = (2.0 * DEPTH) ** 0.25
LOG2E = math.log2(math.e)
Q_PRESCALE = HEAD_DIM ** -0.5 * LOG2E
MASKED = -1e30
SAFE_LOG2_SPAN = 48.0
BF16_ROUNDING_SLACK = 1.02

ATTN_COLUMN_GROUP = 512

V7X_VMEM_LIMIT_BYTES = 56 * 1024 * 1024
V7X_F32_TILE = (8, 128)

BF16 = jnp.bfloat16
F32 = jnp.float32


def _params(sem):
    return pltpu.CompilerParams(dimension_semantics=sem, vmem_limit_bytes=V7X_VMEM_LIMIT_BYTES)


def _const_spec(shape):
    nd = len(shape)
    return pl.BlockSpec(shape, lambda *_: (0,) * nd, pipeline_mode=pl.Buffered(1))


HEADS_PER_DOT = 2


def _qkv_kernel(x_ref, wt_ref, *refs, norm_rope, bq):
    if norm_rope:
        cos_ref, sin_ref, gq_ref, gk_ref, qt_ref, k_ref, vt_ref = refs
    else:
        qt_ref, k_ref, vt_ref, norm_ref = refs
    tm = x_ref.shape[1]
    xb = x_ref[0].astype(BF16)
    max_sq = {"q": None, "k": None}

    def track_norm(name, slab):
        sq = jnp.sum(slab * slab, axis=0, keepdims=True)
        max_sq[name] = sq if max_sq[name] is None else jnp.maximum(max_sq[name], sq)

    def norm_rope_slab(y, gain_ref):
        r = lax.rsqrt(jnp.mean(y * y, axis=0, keepdims=True) + RMS_EPS)
        z = y * gain_ref[...]
        h = AXIS_DIM // 2
        partner = jnp.concatenate([z[h:2 * h], z[0:h], z[3 * h:4 * h], z[2 * h:3 * h]], axis=0)
        return (z * cos_ref[...] + partner * sin_ref[...]) * r

    rows = HEADS_PER_DOT * HEAD_DIM
    for grp in range(QKV_DIM // rows):
        yt = lax.dot_general(wt_ref[grp * rows:(grp + 1) * rows, :], xb, (((1,), (1,)), ((), ())),
                             preferred_element_type=F32)
        for sub in range(HEADS_PER_DOT):
            slab = yt[sub * HEAD_DIM:(sub + 1) * HEAD_DIM]
            head = grp * HEADS_PER_DOT + sub
            if head < N_HEADS:
                if norm_rope:
                    slab = norm_rope_slab(slab, gq_ref)
                else:
                    slab = slab * Q_PRESCALE
                    track_norm("q", slab)
                slab = slab.astype(BF16)
                kvh, g = divmod(head, GQA_GROUP)
                for j in range(tm // bq):
                    col = (j * GQA_GROUP + g) * bq
                    qt_ref[0, kvh, :, col:col + bq] = slab[:, j * bq:(j + 1) * bq]
            elif head < N_HEADS + N_KV_HEADS:
                kvh = head - N_HEADS
                if norm_rope:
                    slab = norm_rope_slab(slab, gk_ref)
                else:
                    track_norm("k", slab)
                k_ref[0, kvh] = slab.T.astype(BF16)
            else:
                kvh = head - N_HEADS - N_KV_HEADS
                vt_ref[0, kvh * HEAD_DIM:(kvh + 1) * HEAD_DIM, :] = slab.astype(BF16)
    if not norm_rope:
        row = lax.broadcasted_iota(jnp.int32, norm_ref.shape[2:], 0)
        norm_ref[0, 0] = jnp.where(row == 0, jnp.max(max_sq["q"], axis=1, keepdims=True),
                                   jnp.max(max_sq["k"], axis=1, keepdims=True))


def _qkv_project(x, wt, rope, *, bq, tm):
    B, S, _ = x.shape
    kern = functools.partial(_qkv_kernel, norm_rope=bool(rope), bq=bq)
    rope_specs = []
    if rope:
        table = pl.BlockSpec((HEAD_DIM, tm), lambda b, i: (0, i))
        rope_specs = [table, table, _const_spec((HEAD_DIM, 1)), _const_spec((HEAD_DIM, 1))]
    out_specs = [
        pl.BlockSpec((1, N_KV_HEADS, HEAD_DIM, GQA_GROUP * tm), lambda b, i: (b, 0, 0, i)),
        pl.BlockSpec((1, N_KV_HEADS, tm, HEAD_DIM), lambda b, i: (b, 0, i, 0)),
        pl.BlockSpec((1, KV_DIM, tm), lambda b, i: (b, 0, i)),
    ]
    out_shape = [
        jax.ShapeDtypeStruct((B, N_KV_HEADS, HEAD_DIM, GQA_GROUP * S), BF16),
        jax.ShapeDtypeStruct((B, N_KV_HEADS, S, HEAD_DIM), BF16),
        jax.ShapeDtypeStruct((B, KV_DIM, S), BF16),
    ]
    if not rope:
        out_specs.append(pl.BlockSpec((1, 1) + V7X_F32_TILE, lambda b, i: (b, i, 0, 0)))
        out_shape.append(jax.ShapeDtypeStruct((B, S // tm) + V7X_F32_TILE, F32))
    return pl.pallas_call(
        kern,
        grid=(B, S // tm),
        in_specs=[
            pl.BlockSpec((1, tm, D_MODEL), lambda b, i: (b, i, 0)),
            _const_spec((QKV_DIM, D_MODEL)),
        ] + rope_specs,
        out_specs=out_specs,
        out_shape=out_shape,
        compiler_params=_params(("parallel", "parallel")),
        name="qkv_project",
    )(x, wt, *rope)


def _store_heads(o_ref, out_t, bq):
    for g in range(GQA_GROUP):
        o_ref[0, :, g * HEAD_DIM:(g + 1) * HEAD_DIM] = out_t[:, g * bq:(g + 1) * bq].T.astype(BF16)


def _global_attn_kernel(qt_ref, k_ref, vt_ref, o_ref, acc_ref, *, bq, bkc, n_split):
    S = k_ref.shape[2]
    nq = GQA_GROUP * bq
    w = nq // n_split
    acc_ref[...] = jnp.zeros_like(acc_ref)

    def body(c, carry):
        ms, ls = carry
        start = pl.multiple_of(c * bkc, bkc)
        k = k_ref[0, 0, pl.ds(start, bkc), :]
        vt = vt_ref[0, :, pl.ds(start, bkc)]
        new_ms, new_ls = [], []
        for h in range(n_split):
            q = qt_ref[0, 0, :, h * w:(h + 1) * w]
            s = jnp.dot(k, q, preferred_element_type=F32)
            m_new = jnp.maximum(ms[h], jnp.max(s, axis=0, keepdims=True))
            alpha = jnp.exp2(ms[h] - m_new)
            p = jnp.exp2(s - m_new)
            new_ls.append(alpha * ls[h] + jnp.sum(p, axis=0, keepdims=True))
            new_ms.append(m_new)
            pv = jnp.dot(vt, p.astype(BF16), preferred_element_type=F32)
            acc_ref[:, h * w:(h + 1) * w] = alpha * acc_ref[:, h * w:(h + 1) * w] + pv
        return tuple(new_ms), tuple(new_ls)

    init = (tuple(jnp.full((1, w), MASKED, F32) for _ in range(n_split)),
            tuple(jnp.zeros((1, w), F32) for _ in range(n_split)))
    _, ls = lax.fori_loop(0, S // bkc, body, init)
    l = jnp.concatenate(ls, axis=1)
    _store_heads(o_ref, acc_ref[...] * (1.0 / l), bq)


def _global_attn_bounded_kernel(qt_ref, k_ref, vt_ref, o_ref, acc_ref, *, bq, bkc, n_split,
                                unroll):
    S = k_ref.shape[2]
    nq = GQA_GROUP * bq
    w = nq // n_split
    sub = V7X_F32_TILE[0]
    acc_ref[...] = jnp.zeros_like(acc_ref)

    def body(c, ls):
        start = pl.multiple_of(c * bkc, bkc)
        k = k_ref[0, 0, pl.ds(start, bkc), :]
        vt = vt_ref[0, :, pl.ds(start, bkc)]
        ss = [jnp.dot(k, qt_ref[0, 0, :, h * w:(h + 1) * w], preferred_element_type=F32)
              for h in range(n_split)]
        new_ls = []
        for h in range(n_split):
            cols = slice(h * w, (h + 1) * w)
            p = jnp.exp2(ss[h])
            new_ls.append(ls[h] + jnp.sum(p.reshape(bkc // sub, sub, w), axis=0))
            acc_ref[:, cols] += jnp.dot(vt, p.astype(BF16), preferred_element_type=F32)
        return tuple(new_ls)

    ls = lax.fori_loop(0, S // bkc, body, tuple(jnp.zeros((sub, w), F32) for _ in range(n_split)),
                       unroll=unroll)
    l = jnp.concatenate([jnp.sum(x, axis=0, keepdims=True) for x in ls], axis=1)
    _store_heads(o_ref, acc_ref[...] * (1.0 / l), bq)


def _global_attention_call(kern, name, qt, k, vt, *, bq):
    B, _, S, _ = k.shape
    nq = GQA_GROUP * bq
    return pl.pallas_call(
        kern,
        grid=(B, N_KV_HEADS, S // bq),
        in_specs=[
            pl.BlockSpec((1, 1, HEAD_DIM, nq), lambda b, h, i: (b, h, 0, i)),
            pl.BlockSpec((1, 1, S, HEAD_DIM), lambda b, h, i: (b, h, 0, 0)),
            pl.BlockSpec((1, HEAD_DIM, S), lambda b, h, i: (b, h, 0)),
        ],
        out_specs=pl.BlockSpec((1, bq, GQA_GROUP * HEAD_DIM), lambda b, h, i: (b, i, h)),
        out_shape=jax.ShapeDtypeStruct((B, S, Q_DIM), BF16),
        scratch_shapes=[pltpu.VMEM((HEAD_DIM, nq), F32)],
        compiler_params=_params(("parallel", "parallel", "arbitrary")),
        name=name,
    )(qt, k, vt)


def _global_attention_bounded(qt, k, vt, *, bq, bkc, n_split, unroll):
    kern = functools.partial(_global_attn_bounded_kernel, bq=bq, bkc=bkc, n_split=n_split,
                             unroll=unroll)
    return _global_attention_call(kern, "global_attention_bounded", qt, k, vt, bq=bq)


def _global_attention(qt, k, vt, *, bq, bkc, n_split):
    kern = functools.partial(_global_attn_kernel, bq=bq, bkc=bkc, n_split=n_split)
    return _global_attention_call(kern, "global_attention", qt, k, vt, bq=bq)


def _window_attn_kernel(qt_ref, k_ref, vt_ref, bias_ref, sink_ref, o_ref, s_ref, *, qb, bounded):
    step = pl.program_id(2)
    last_step = pl.num_programs(2) - 1
    nb = k_ref.shape[2] // WINDOW
    nq = GQA_GROUP * WINDOW
    sink = sink_ref[0]

    def block_start(n):
        return pl.multiple_of(n * WINDOW, WINDOW)

    starts = [[block_start(jnp.maximum(step * qb + qi - 1, 0)), block_start(step * qb + qi),
               block_start(jnp.minimum(step * qb + qi + 1, nb - 1))] for qi in range(qb)]
    for qi in range(qb):
        kw = jnp.concatenate([k_ref[0, 0, pl.ds(st, WINDOW), :] for st in starts[qi]], axis=0)
        s = jnp.dot(kw, qt_ref[0, 0, :, qi * nq:(qi + 1) * nq],
                    preferred_element_type=F32) + bias_ref[0]
        if qi == 0:
            s = jnp.concatenate(
                [jnp.where(step == 0, MASKED, s[:WINDOW]), s[WINDOW:]], axis=0)
        if qi == qb - 1:
            s = jnp.concatenate(
                [s[:2 * WINDOW], jnp.where(step == last_step, MASKED, s[2 * WINDOW:])], axis=0)
        s_ref[qi] = s
    for qi in range(qb):
        vw = jnp.concatenate([vt_ref[0, :, pl.ds(st, WINDOW)] for st in starts[qi]], axis=1)
        if bounded:
            p = jnp.exp2(s_ref[qi])
            l = jnp.sum(p, axis=0, keepdims=True) + jnp.exp2(sink)
        else:
            m = jnp.maximum(jnp.max(s_ref[qi], axis=0, keepdims=True), sink)
            p = jnp.exp2(s_ref[qi] - m)
            l = jnp.sum(p, axis=0, keepdims=True) + jnp.exp2(sink - m)
        out_t = jnp.dot(vw, p.astype(BF16), preferred_element_type=F32) * (1.0 / l)
        for g in range(GQA_GROUP):
            o_ref[0, qi * WINDOW:(qi + 1) * WINDOW, g * HEAD_DIM:(g + 1) * HEAD_DIM] = (
                out_t[:, g * WINDOW:(g + 1) * WINDOW].T.astype(BF16))


def _window_attention(qt, k, vt, bias_t, sink_t, *, qb, bounded):
    B, _, S, _ = k.shape
    nq = GQA_GROUP * WINDOW
    return pl.pallas_call(
        functools.partial(_window_attn_kernel, qb=qb, bounded=bounded),
        grid=(B, N_KV_HEADS, S // (qb * WINDOW)),
        in_specs=[
            pl.BlockSpec((1, 1, HEAD_DIM, qb * nq), lambda b, h, n: (b, h, 0, n)),
            pl.BlockSpec((1, 1, S, HEAD_DIM), lambda b, h, n: (b, h, 0, 0)),
            pl.BlockSpec((1, HEAD_DIM, S), lambda b, h, n: (b, h, 0)),
            pl.BlockSpec((1, 3 * WINDOW, nq), lambda b, h, n: (h, 0, 0)),
            pl.BlockSpec((1, 1, nq), lambda b, h, n: (h, 0, 0)),
        ],
        out_specs=pl.BlockSpec((1, qb * WINDOW, GQA_GROUP * HEAD_DIM), lambda b, h, n: (b, n, h)),
        out_shape=jax.ShapeDtypeStruct((B, S, Q_DIM), BF16),
        scratch_shapes=[pltpu.VMEM((qb, 3 * WINDOW, nq), F32)],
        compiler_params=_params(("parallel", "parallel", "parallel")),
        name="window_attention",
    )(qt, k, vt, bias_t, sink_t)


def _layer_norm(y, g, b):
    mu = jnp.mean(y, axis=-1, keepdims=True)
    d = y - mu
    var = jnp.mean(d * d, axis=-1, keepdims=True)
    return d * lax.rsqrt(var + LN_EPS) * g + b


ROW_GROUPS = 4


def _post_attention_kernel(x_ref, o_ref, wo_ref, g1_ref, b1_ref, wg_ref, wu_ref, wd_ref, g2_ref,
                           b2_ref, y_ref):
    tm = x_ref.shape[0]
    rows = [slice(r * tm // ROW_GROUPS, (r + 1) * tm // ROW_GROUPS) for r in range(ROW_GROUPS)]
    hs = [jnp.dot(o_ref[r, :], wo_ref[...], preferred_element_type=F32) for r in rows]
    x1s = [_layer_norm(DEEPNORM_ALPHA * x_ref[r, :] + h, g1_ref[...], b1_ref[...])
           for r, h in zip(rows, hs)]
    mids = []
    for x1 in x1s:
        xb = x1.astype(BF16)
        gate = jnp.dot(xb, wg_ref[...], preferred_element_type=F32)
        up = jnp.dot(xb, wu_ref[...], preferred_element_type=F32)
        mids.append((gate * jax.nn.sigmoid(gate) * up).astype(BF16))
    hs = [jnp.dot(mid, wd_ref[...], preferred_element_type=F32) for mid in mids]
    for r, x1, h in zip(rows, x1s, hs):
        y_ref[r, :] = _layer_norm(DEEPNORM_ALPHA * x1 + h, g2_ref[...], b2_ref[...])


def _post_attention(x, o, wo, g1, b1, wg, wu, wd, g2, b2, *, layer, tm):
    T = x.shape[0]
    vec = _const_spec((1, D_MODEL))

    def stacked(shape):
        return pl.BlockSpec((None,) + shape, lambda t: (layer, 0, 0), pipeline_mode=pl.Buffered(1))

    return pl.pallas_call(
        _post_attention_kernel,
        grid=(T // tm,),
        in_specs=[
            pl.BlockSpec((tm, D_MODEL), lambda i: (i, 0)),
            pl.BlockSpec((tm, Q_DIM), lambda i: (i, 0)),
            _const_spec((Q_DIM, D_MODEL)), vec, vec,
            stacked((D_MODEL, D_FF)),
            stacked((D_MODEL, D_FF)),
            stacked((D_FF, D_MODEL)), vec, vec,
        ],
        out_specs=pl.BlockSpec((tm, D_MODEL), lambda i: (i, 0)),
        out_shape=jax.ShapeDtypeStruct((T, D_MODEL), F32),
        compiler_params=_params(("parallel",)),
        name="post_attention",
    )(x, o, wo, g1, b1, wg, wu, wd, g2, b2)


def _rope_tables_t(seq_len):
    rows_n = seq_len // GRID_W
    inv_freq = ROPE_THETA ** (-jnp.arange(0, AXIS_DIM, 2, dtype=F32) / AXIS_DIM)
    ang_r = (jnp.arange(rows_n, dtype=F32)[:, None] * inv_freq).T
    ang_c = (jnp.arange(GRID_W, dtype=F32)[:, None] * inv_freq).T
    over_rows = lambda a: jnp.repeat(a, GRID_W, axis=1)
    over_cols = lambda a: jnp.tile(a, (1, rows_n))
    cos_r, sin_r = over_rows(jnp.cos(ang_r)), over_rows(jnp.sin(ang_r))
    cos_c, sin_c = over_cols(jnp.cos(ang_c)), over_cols(jnp.sin(ang_c))
    cos_t = jnp.concatenate([cos_r, cos_r, cos_c, cos_c], axis=0)
    sin_t = jnp.concatenate([-sin_r, sin_r, -sin_c, sin_c], axis=0)
    return cos_t, sin_t


def _t5_bucket(rel):
    nb = N_BUCKETS // 2
    max_exact = nb // 2
    base = (rel > 0).astype(jnp.int32) * nb
    n = jnp.abs(rel)
    nf = jnp.maximum(n, max_exact).astype(F32)
    large = max_exact + (jnp.log(nf / max_exact) / math.log(MAX_DISTANCE / max_exact)
                         * (nb - max_exact)).astype(jnp.int32)
    large = jnp.minimum(large, nb - 1)
    return base + jnp.where(n < max_exact, n, large)


def _window_bias_t(rel_bias_table):
    C = 3 * WINDOW
    n_diag = C + WINDOW - 1
    rel = jnp.arange(n_diag + 1) - (2 * WINDOW - 1)
    bucket = _t5_bucket(rel)[:, None]
    table = rel_bias_table.astype(F32)
    f = sum(jnp.where(bucket == b, table[b], 0.0) for b in range(N_BUCKETS)) * LOG2E
    f = jnp.where((jnp.abs(rel) <= WINDOW)[:, None], f, MASKED)
    skew = jnp.tile(f, (WINDOW, 1))[:WINDOW * n_diag].reshape(WINDOW, n_diag, N_HEADS)
    bias = skew[:, WINDOW - 1:WINDOW - 1 + C]
    bias = bias.transpose(2, 1, 0).reshape(N_KV_HEADS, GQA_GROUP, C, WINDOW)
    return bias.transpose(0, 2, 1, 3).reshape(N_KV_HEADS, C, GQA_GROUP * WINDOW)


def _trunk(x, a_wt, a_q_gain, a_k_gain, a_wo, b_wt, b_sink, b_wo, rel_bias_table,
           ln1_g, ln1_b, wg, wu, wd, ln2_g, ln2_b):
    B, S, _ = x.shape
    T = B * S
    tm_qkv = min(2048, S)
    tm_tok = min(1024, T)
    bq = S if S <= 2048 else 1024
    bkc = min(1024, S)
    qb = min(16, S // WINDOW)
    cos_t, sin_t = _rope_tables_t(S)
    for i in range(DEPTH):
        j = i // 2
        if i % 2 == 0:
            gq = a_q_gain[j].astype(F32) * Q_PRESCALE
            gk = a_k_gain[j].astype(F32)
            qt, k, vt = _qkv_project(x, a_wt[j], (cos_t, sin_t, gq[:, None], gk[:, None]),
                                     bq=bq, tm=tm_qkv)
            bound = HEAD_DIM * jnp.max(jnp.abs(gq)) * jnp.max(jnp.abs(gk)) * BF16_ROUNDING_SLACK
            n_split = GQA_GROUP * bq // ATTN_COLUMN_GROUP
            attn = functools.partial(_global_attention, bq=bq, bkc=512, n_split=n_split)
            attn_bounded = functools.partial(_global_attention_bounded, bq=bq, bkc=bkc,
                                             n_split=n_split, unroll=8)
            o = lax.cond(bound <= SAFE_LOG2_SPAN, attn_bounded, attn, qt, k, vt)
            wo = a_wo[j]
        else:
            qt, k, vt, norms = _qkv_project(x, b_wt[j], (), bq=WINDOW, tm=tm_qkv)
            bias_t = _window_bias_t(rel_bias_table)
            sink_t = jnp.repeat(b_sink[j].astype(F32) * LOG2E, WINDOW).reshape(
                N_KV_HEADS, 1, GQA_GROUP * WINDOW)
            qk_bound = jnp.sqrt(jnp.max(norms[:, :, 0]) * jnp.max(norms[:, :, 1])) * BF16_ROUNDING_SLACK
            bias_bound = jnp.max(jnp.where(bias_t > MASKED, jnp.abs(bias_t), 0.0))
            bounded = ((qk_bound + bias_bound <= SAFE_LOG2_SPAN)
                       & (jnp.max(jnp.abs(sink_t)) <= SAFE_LOG2_SPAN))
            window = functools.partial(_window_attention, qb=qb)
            o = lax.cond(bounded, functools.partial(window, bounded=True),
                         functools.partial(window, bounded=False), qt, k, vt, bias_t, sink_t)
            wo = b_wo[j]
        x2 = _post_attention(x.reshape(T, D_MODEL), o.reshape(T, Q_DIM), wo,
                             ln1_g[i][None], ln1_b[i][None], wg, wu, wd,
                             ln2_g[i][None], ln2_b[i][None], layer=i, tm=tm_tok)
        x = x2.reshape(B, S, D_MODEL)
    return x


def _prepare_weights(a_w_qkv, a_q_gain, a_k_gain, a_w_o, b_w_qkv, b_sink, b_w_o, rel_bias_table,
                     ln1_g, ln1_b, w_gate, w_up, w_down, ln2_g, ln2_b):
    return (jnp.swapaxes(a_w_qkv.astype(BF16), 1, 2), a_q_gain, a_k_gain, a_w_o.astype(BF16),
            jnp.swapaxes(b_w_qkv.astype(BF16), 1, 2), b_sink, b_w_o.astype(BF16),
            rel_bias_table, ln1_g, ln1_b, w_gate.astype(BF16), w_up.astype(BF16),
            w_down.astype(BF16), ln2_g, ln2_b)


def kernel(x_prompt, x_sample, a_w_qkv, a_q_gain, a_k_gain, a_w_o, b_w_qkv, b_sink, b_w_o,
           rel_bias_table, ln1_g, ln1_b, w_gate, w_up, w_down, ln2_g, ln2_b):
    weights = _prepare_weights(a_w_qkv, a_q_gain, a_k_gain, a_w_o, b_w_qkv, b_sink, b_w_o,
                               rel_bias_table, ln1_g, ln1_b, w_gate, w_up, w_down, ln2_g, ln2_b)
    return (_trunk(x_prompt, *weights), _trunk(x_sample, *weights))
```
